```python
import math
import jax, jax.numpy as jnp
from jax import lax
import numpy as np

D_MODEL = 4096
BATCH = 2
SEQ = 4096
DEPTH = 1
DEC_BATCH = 32
DEC_SEQ = 64
PAST_LEN = 1024

CHUNK = 64
D_MIX = D_MODEL
D_ATTN = D_MIX // 2
D_SSM = D_MIX - D_ATTN
HEAD_DIM = 128
N_HEADS = D_ATTN // HEAD_DIM
SSM_GROUP = 16
N_GROUPS = D_SSM // SSM_GROUP
SSM_STATE = 64
D_FF = 256 * ((8 * D_MODEL // 3 + 255) // 256)
N_IN = 3 * D_ATTN + N_HEADS + D_SSM
Q_BLOCK = 128
EPS = 1e-6
NEG_INF = -1e30
ATTN_SCALE = HEAD_DIM ** -0.5
FORGET_BIAS_MEAN = 2.0
DT_MIN = 0.001
DT_MAX = 0.1

kernel_name = 'hymba_fox_s5_macaron_stream_step'


def rms_norm(x, g):
    xf = x.astype(jnp.float32)
    y = xf * lax.rsqrt(jnp.mean(xf * xf, axis=-1, keepdims=True) + EPS)
    return (y * g.astype(jnp.float32)).astype(x.dtype)


def swiglu(h, w1, w3, w2):
    return (jax.nn.silu(h @ w1) * (h @ w3)) @ w2


def combined_projection(h, w_in, b_f, q_norm, k_norm):
    b, s, _ = h.shape
    z = h @ w_in
    q = rms_norm(z[..., :D_ATTN].reshape(b, s, N_HEADS, HEAD_DIM), q_norm)
    k = rms_norm(z[..., D_ATTN:2 * D_ATTN].reshape(b, s, N_HEADS, HEAD_DIM), k_norm)
    v = z[..., 2 * D_ATTN:3 * D_ATTN].reshape(b, s, N_HEADS, HEAD_DIM)
    logf = jax.nn.log_sigmoid((z[..., 3 * D_ATTN:3 * D_ATTN + N_HEADS] + b_f).astype(jnp.float32))
    u = z[..., 3 * D_ATTN + N_HEADS:].reshape(b, s, N_GROUPS, SSM_GROUP)
    return q, k, v, logf, u


def fox_attend(q, k, v, f_q, f_k, q_pos, k_pos):
    s = jnp.einsum('bqhd,bkhd->bhqk', q, k, preferred_element_type=jnp.float32) * ATTN_SCALE
    s = s + jnp.swapaxes(f_q, 1, 2)[:, :, :, None] - jnp.swapaxes(f_k, 1, 2)[:, :, None, :]
    s = jnp.where(k_pos[None, :] <= q_pos[:, None], s, NEG_INF)
    p = jax.nn.softmax(s, axis=-1)
    return jnp.einsum('bhqk,bkhd->bqhd', p.astype(v.dtype), v)


def fox_prompt(q, k, v, f):
    b, s = q.shape[:2]
    nb = s // Q_BLOCK
    qb = q.reshape(b, nb, Q_BLOCK, N_HEADS, HEAD_DIM).transpose(1, 0, 2, 3, 4)
    fb = f.reshape(b, nb, Q_BLOCK, N_HEADS).transpose(1, 0, 2, 3)
    k_pos = jnp.arange(s)

    def one_block(args):
        q_i, f_i, i = args
        q_pos = i * Q_BLOCK + jnp.arange(Q_BLOCK)
        return fox_attend(q_i, k, v, f_i, f, q_pos, k_pos)

    o = lax.map(one_block, (qb, fb, jnp.arange(nb)))
    return o.transpose(1, 0, 2, 3, 4).reshape(b, s, N_HEADS, HEAD_DIM)


def _ssm_combine(left, right):
    a_l, b_l = left
    a_r, b_r = right
    return a_l * a_r, a_r * b_l + b_r


def s5_mixer(u, x0, lam_re, lam_im, log_dt, b_re, b_im, c_re, c_im, d_skip, w_glu):
    bsz, s = u.shape[:2]
    f32 = jnp.float32
    lam = lax.complex(lam_re.astype(f32), lam_im.astype(f32))
    dt = jnp.exp(log_dt.astype(f32))[:, None]
    lam_bar = jnp.exp(lam * dt)
    b_bar = ((lam_bar - 1.0) / lam)[:, :, None] * lax.complex(b_re.astype(f32), b_im.astype(f32))
    c = lax.complex(c_re.astype(f32), c_im.astype(f32))
    uf = u.astype(f32)
    bu = jnp.einsum('gpc,bsgc->bsgp', b_bar, uf.astype(jnp.complex64))
    a = jnp.broadcast_to(lam_bar, (1, s) + lam_bar.shape)
    a_cum, xs = lax.associative_scan(_ssm_combine, (a, bu), axis=1)
    if x0 is not None:
        xs = xs + a_cum * x0[:, None]
    y = jnp.real(jnp.einsum('gcp,bsgp->bsgc', c, xs)) + d_skip.astype(f32).reshape(N_GROUPS, SSM_GROUP) * uf
    y = jax.nn.gelu(y.reshape(bsz, s, D_SSM))
    y = y * jax.nn.sigmoid(y @ w_glu.astype(f32))
    return y.astype(u.dtype), xs[:, -1]


def merge_heads(o_attn, y_ssm, g_attn, g_ssm, w_out):
    return jnp.concatenate([rms_norm(o_attn, g_attn), rms_norm(y_ssm, g_ssm)], axis=-1) @ w_out


def setup_inputs(seed: int = 0) -> dict:
    key = jax.random.key(seed)
    ks = jax.random.split(key, 32)
    f32 = jnp.float32
    L = DEPTH

    def nrm(k, shape, scale):
        return jax.random.normal(k, shape, f32) * scale

    def gain(k, shape):
        return 1.0 + 0.01 * jax.random.normal(k, shape, f32)

    return {
        'x_prompt': nrm(ks[0], (BATCH, SEQ, D_MODEL), 1.0),
        'x_sample': nrm(ks[1], (DEC_BATCH, DEC_SEQ, D_MODEL), 1.0),
        'cache_k': nrm(ks[2], (L, DEC_BATCH, PAST_LEN, N_HEADS, HEAD_DIM), 1.0),
        'cache_v': nrm(ks[3], (L, DEC_BATCH, PAST_LEN, N_HEADS, HEAD_DIM), 1.0),
        'cache_logf': jax.nn.log_sigmoid(FORGET_BIAS_MEAN + jax.random.normal(ks[4], (L, DEC_BATCH, PAST_LEN, N_HEADS), f32)),
        'state_ssm_re': nrm(ks[5], (L, DEC_BATCH, N_GROUPS, SSM_STATE), 0.5),
        'state_ssm_im': nrm(ks[6], (L, DEC_BATCH, N_GROUPS, SSM_STATE), 0.5),
        'norm_ffn1': gain(ks[7], (L, D_MODEL)),
        'w1_a': nrm(ks[8], (L, D_MODEL, D_FF), D_MODEL ** -0.5),
        'w3_a': nrm(ks[9], (L, D_MODEL, D_FF), D_MODEL ** -0.5),
        'w2_a': nrm(ks[10], (L, D_FF, D_MODEL), D_FF ** -0.5),
        'norm_mix': gain(ks[11], (L, D_MODEL)),
        'w_in': nrm(ks[12], (L, D_MODEL, N_IN), D_MODEL ** -0.5),
        'b_f': FORGET_BIAS_MEAN + nrm(ks[13], (L, N_HEADS), 0.1),
        'q_norm': gain(ks[14], (L, HEAD_DIM)),
        'k_norm': gain(ks[15], (L, HEAD_DIM)),
        'lam_re': -0.5 + nrm(ks[16], (L, N_GROUPS, SSM_STATE), 0.01),
        'lam_im': math.pi * jnp.arange(SSM_STATE, dtype=f32) + nrm(ks[17], (L, N_GROUPS, SSM_STATE), 0.01),
        'log_dt': jax.random.uniform(ks[18], (L, N_GROUPS), f32, math.log(DT_MIN), math.log(DT_MAX)),
        'b_re': nrm(ks[19], (L, N_GROUPS, SSM_STATE, SSM_GROUP), (2 * SSM_GROUP) ** -0.5),
        'b_im': nrm(ks[20], (L, N_GROUPS, SSM_STATE, SSM_GROUP), (2 * SSM_GROUP) ** -0.5),
        'c_re': nrm(ks[21], (L, N_GROUPS, SSM_GROUP, SSM_STATE), (2 * SSM_STATE) ** -0.5),
        'c_im': nrm(ks[22], (L, N_GROUPS, SSM_GROUP, SSM_STATE), (2 * SSM_STATE) ** -0.5),
        'd_skip': nrm(ks[23], (L, D_SSM), 1.0),
        'w_glu': nrm(ks[24], (L, D_SSM, D_SSM), D_SSM ** -0.5),
        'out_norm_attn': gain(ks[25], (L, D_ATTN)),
        'out_norm_ssm': gain(ks[26], (L, D_SSM)),
        'w_out': nrm(ks[27], (L, D_MIX, D_MODEL), D_MIX ** -0.5),
        'norm_ffn2': gain(ks[28], (L, D_MODEL)),
        'w1_b': nrm(ks[29], (L, D_MODEL, D_FF), D_MODEL ** -0.5),
        'w3_b': nrm(ks[30], (L, D_MODEL, D_FF), D_MODEL ** -0.5),
        'w2_b': nrm(ks[31], (L, D_FF, D_MODEL), D_FF ** -0.5),
    }


def reference(x_prompt, x_sample, cache_k, cache_v, cache_logf, state_ssm_re, state_ssm_im,
              norm_ffn1, w1_a, w3_a, w2_a, norm_mix, w_in, b_f, q_norm, k_norm,
              lam_re, lam_im, log_dt, b_re, b_im, c_re, c_im, d_skip, w_glu,
              out_norm_attn, out_norm_ssm, w_out, norm_ffn2, w1_b, w3_b, w2_b):
    f32 = jnp.float32
    xp, xs = x_prompt, x_sample
    bp, s_len = xp.shape[:2]
    t_len = xs.shape[1]
    past = cache_k.shape[2]
    kp_l, vp_l, fp_l, srp_l, sip_l = [], [], [], [], []
    ks_l, vs_l, fs_l, srs_l, sis_l = [], [], [], [], []
    for l in range(DEPTH):
        ssm_w = (lam_re[l], lam_im[l], log_dt[l], b_re[l], b_im[l], c_re[l], c_im[l], d_skip[l], w_glu[l])
        xp = xp + 0.5 * swiglu(rms_norm(xp, norm_ffn1[l]), w1_a[l], w3_a[l], w2_a[l])
        xs = xs + 0.5 * swiglu(rms_norm(xs, norm_ffn1[l]), w1_a[l], w3_a[l], w2_a[l])

        q, k, v, logf, u = combined_projection(rms_norm(xp, norm_mix[l]), w_in[l], b_f[l], q_norm[l], k_norm[l])
        o = fox_prompt(q, k, v, jnp.cumsum(logf, axis=1))
        y, last = s5_mixer(u, None, *ssm_w)
        xp = xp + merge_heads(o.reshape(bp, s_len, D_ATTN), y, out_norm_attn[l], out_norm_ssm[l], w_out[l])
        kp_l.append(k)
        vp_l.append(v)
        fp_l.append(logf)
        srp_l.append(jnp.real(last))
        sip_l.append(jnp.imag(last))

        q, k, v, logf, u = combined_projection(rms_norm(xs, norm_mix[l]), w_in[l], b_f[l], q_norm[l], k_norm[l])
        k_all = jnp.concatenate([cache_k[l].astype(k.dtype), k], axis=1)
        v_all = jnp.concatenate([cache_v[l].astype(v.dtype), v], axis=1)
        f_all = jnp.cumsum(jnp.concatenate([cache_logf[l].astype(f32), logf], axis=1), axis=1)
        o = fox_attend(q, k_all, v_all, f_all[:, past:], f_all, past + jnp.arange(t_len), jnp.arange(past + t_len))
        x0 = lax.complex(state_ssm_re[l].astype(f32), state_ssm_im[l].astype(f32))
        y, last = s5_mixer(u, x0, *ssm_w)
        xs = xs + merge_heads(o.reshape(xs.shape[0], t_len, D_ATTN), y, out_norm_attn[l], out_norm_ssm[l], w_out[l])
        ks_l.append(k)
        vs_l.append(v)
        fs_l.append(logf)
        srs_l.append(jnp.real(last))
        sis_l.append(jnp.imag(last))

        xp = xp + 0.5 * swiglu(rms_norm(xp, norm_ffn2[l]), w1_b[l], w3_b[l], w2_b[l])
        xs = xs + 0.5 * swiglu(rms_norm(xs, norm_ffn2[l]), w1_b[l], w3_b[l], w2_b[l])

    k_prompt = jnp.stack(kp_l, 0)
    v_prompt = jnp.stack(vp_l, 0)
    logf_prompt = jnp.stack(fp_l, 0)
    ssm_re_prompt = jnp.stack(srp_l, 0)
    ssm_im_prompt = jnp.stack(sip_l, 0)
    k_sample = jnp.stack(ks_l, 0)
    v_sample = jnp.stack(vs_l, 0)
    logf_sample = jnp.stack(fs_l, 0)
    ssm_re_sample = jnp.stack(srs_l, 0)
    ssm_im_sample = jnp.stack(sis_l, 0)
    return (xp, xs, k_prompt, v_prompt, logf_prompt, ssm_re_prompt, ssm_im_prompt,
            k_sample, v_sample, logf_sample, ssm_re_sample, ssm_im_sample)
```

```python
import functools
import math

import jax
import jax.numpy as jnp
import numpy as np
from jax import lax
from jax.experimental import pallas as pl
from jax.experimental.pallas import tpu as pltpu

F32 = jnp.float32
BF16 = jnp.bfloat16

EPS = 1e-6
NEG_INF = -1e30
HEAD_DIM = 128
SSM_GROUP = 16
SSM_STATE = 64
LANES = 128
CHUNK_T = LANES // SSM_GROUP
GROUPS_PER_STEP = 8
VMEM_LIMIT = 56 * 1024 * 1024


def _cparams(sem):
    return pltpu.CompilerParams(dimension_semantics=sem, vmem_limit_bytes=VMEM_LIMIT)


def _rms(x, g):
    ms = jnp.mean(x * x, axis=-1, keepdims=True)
    return x * lax.rsqrt(ms + EPS) * g


def _dot(a, b):
    return jnp.dot(a, b, preferred_element_type=F32)


def _dot_nt(a, b):
    return lax.dot_general(a, b, (((1,), (1,)), ((), ())), preferred_element_type=F32)


def _ffn_kernel(x_ref, g_ref, w1_ref, w3_ref, w2_ref, o_ref, h_ref):
    @pl.when(pl.program_id(1) == 0)
    def _():
        x = x_ref[...]
        h_ref[...] = _rms(x, g_ref[...]).astype(BF16)
        o_ref[...] = x

    h = h_ref[...]
    a = _dot(h, w1_ref[...])
    b = _dot(h, w3_ref[...])
    act = (0.5 * a * jax.nn.sigmoid(a) * b).astype(BF16)
    o_ref[...] += _dot(act, w2_ref[...])


def _ffn(x, g, w1, w3, w2, *, tm, tf):
    t, d = x.shape
    f = w1.shape[1]
    tm, tf = min(tm, t), min(tf, f)
    return pl.pallas_call(
        _ffn_kernel,
        grid=(t // tm, f // tf),
        in_specs=[
            pl.BlockSpec((tm, d), lambda i, j: (i, 0), pipeline_mode=pl.Buffered(1)),
            pl.BlockSpec((1, d), lambda i, j: (0, 0)),
            pl.BlockSpec((d, tf), lambda i, j: (0, j)),
            pl.BlockSpec((d, tf), lambda i, j: (0, j)),
            pl.BlockSpec((tf, d), lambda i, j: (j, 0)),
        ],
        out_specs=pl.BlockSpec((tm, d), lambda i, j: (i, 0)),
        out_shape=jax.ShapeDtypeStruct((t, d), F32),
        scratch_shapes=[pltpu.VMEM((tm, d), BF16)],
        compiler_params=_cparams(("parallel", "arbitrary")),
        name="ffn",
    )(x, g.reshape(1, d), w1, w3, w2)


def _inproj_kernel(x_ref, g_ref, w_ref, wf_ref, bf_ref, qn_ref, kn_ref,
                   q_ref, k_ref, kb_ref, v_ref, vb_ref, u_ref, lf_ref, h_ref, *, tiles, heads):
    n = pl.program_id(1)

    @pl.when(n == 0)
    def _():
        h = _rms(x_ref[...], g_ref[...]).astype(BF16)
        h_ref[...] = h
        zf = _dot(h, wf_ref[...]) + bf_ref[...]
        lf_ref[...] = jnp.minimum(zf, 0.0) - jnp.log1p(jnp.exp(-jnp.abs(zf)))

    z = _dot(h_ref[...], w_ref[...])
    sec = n // tiles

    def head_norm(gain):
        outs = []
        for hh in range(heads):
            outs.append(_rms(z[:, hh * HEAD_DIM:(hh + 1) * HEAD_DIM], gain))
        return jnp.concatenate(outs, axis=-1) if heads > 1 else outs[0]

    @pl.when(sec == 0)
    def _():
        q_ref[...] = head_norm(qn_ref[...]).astype(BF16)

    @pl.when(sec == 1)
    def _():
        kk = head_norm(kn_ref[...])
        k_ref[...] = kk
        kb_ref[...] = kk.astype(BF16)

    @pl.when(sec == 2)
    def _():
        v_ref[...] = z
        vb_ref[...] = z.astype(BF16)

    @pl.when(sec == 3)
    def _():
        u_ref[...] = z


def _inproj(x, g, w_main, w_f, b_f, q_norm, k_norm, *, tm, tn):
    t, d = x.shape
    da = w_main.shape[1] // 4
    nh = w_f.shape[1]
    tm, tn = min(tm, t), min(tn, da)
    tiles = da // tn

    def sec_map(s):
        return lambda i, n: (i, jnp.clip(n - s * tiles, 0, tiles - 1))

    row = lambda i, n: (0, 0)
    f32_out = jax.ShapeDtypeStruct((t, da), F32)
    bf_out = jax.ShapeDtypeStruct((t, da), BF16)
    return pl.pallas_call(
        functools.partial(_inproj_kernel, tiles=tiles, heads=tn // HEAD_DIM),
        grid=(t // tm, 4 * tiles),
        in_specs=[
            pl.BlockSpec((tm, d), lambda i, n: (i, 0)),
            pl.BlockSpec((1, d), row),
            pl.BlockSpec((d, tn), lambda i, n: (0, n)),
            pl.BlockSpec((d, nh), row),
            pl.BlockSpec((1, nh), row),
            pl.BlockSpec((1, HEAD_DIM), row),
            pl.BlockSpec((1, HEAD_DIM), row),
        ],
        out_specs=[
            pl.BlockSpec((tm, tn), sec_map(0)),
            pl.BlockSpec((tm, tn), sec_map(1)),
            pl.BlockSpec((tm, tn), sec_map(1)),
            pl.BlockSpec((tm, tn), sec_map(2)),
            pl.BlockSpec((tm, tn), sec_map(2)),
            pl.BlockSpec((tm, tn), sec_map(3)),
            pl.BlockSpec((tm, nh), lambda i, n: (i, 0)),
        ],
        out_shape=[bf_out, f32_out, bf_out, f32_out, bf_out, f32_out,
                   jax.ShapeDtypeStruct((t, nh), F32)],
        scratch_shapes=[pltpu.VMEM((tm, d), BF16)],
        compiler_params=_cparams(("parallel", "arbitrary")),
        name="inproj",
    )(x, g.reshape(1, d), w_main, w_f, b_f.reshape(1, nh),
      q_norm.reshape(1, HEAD_DIM), k_norm.reshape(1, HEAD_DIM))


def _split3(x):
    hi = x.astype(BF16)
    r1 = x - hi.astype(F32)
    mid = r1.astype(BF16)
    lo = (r1 - mid.astype(F32)).astype(BF16)
    return hi, mid, lo


def _cumsum_kernel(x_ref, o_ref, carry_ref):
    @pl.when(pl.program_id(0) == 0)
    def _():
        carry_ref[...] = jnp.zeros_like(carry_ref)

    x = x_ref[...]
    bs = x.shape[1]
    row = lax.broadcasted_iota(jnp.int32, (bs, bs), 0)
    col = lax.broadcasted_iota(jnp.int32, (bs, bs), 1)
    tri = jnp.where(row <= col, 1.0, 0.0).astype(BF16)
    hi, mid, lo = _split3(x)
    c = _dot(hi, tri) + _dot(mid, tri) + _dot(lo, tri) + carry_ref[...]
    o_ref[...] = c
    carry_ref[...] = c[:, bs - 1:bs]


def _cumsum_rows(x, *, bs):
    r, s = x.shape
    bs = min(bs, s)
    return pl.pallas_call(
        _cumsum_kernel,
        grid=(s // bs,),
        in_specs=[pl.BlockSpec((r, bs), lambda j: (0, j))],
        out_specs=pl.BlockSpec((r, bs), lambda j: (0, j)),
        out_shape=jax.ShapeDtypeStruct((r, s), F32),
        scratch_shapes=[pltpu.VMEM((r, 1), F32)],
        compiler_params=_cparams(("arbitrary",)),
        name="cumsum",
    )(x)


def _softmax_step(s, v, m_prev, l_prev, acc_prev):
    m_new = jnp.maximum(m_prev, jnp.max(s, axis=-1, keepdims=True))
    alpha = jnp.exp(m_prev - m_new)
    p = jnp.exp(s - m_new)
    l_new = alpha * l_prev + jnp.sum(p, axis=-1, keepdims=True)
    acc_new = alpha * acc_prev + _dot(p.astype(BF16), v)
    return m_new, l_new, acc_new


def _attn_prompt_kernel(qi_tab, ki_tab, q_ref, k_ref, v_ref, ccol_ref, crow_ref, o_ref,
                        m_ref, l_ref, acc_ref, *, heads, scale):
    t = pl.program_id(2)
    qi, ki = qi_tab[t], ki_tab[t]
    tq, tk = q_ref.shape[1], k_ref.shape[1]

    @pl.when(ki == 0)
    def _():
        m_ref[...] = jnp.full_like(m_ref, NEG_INF)
        l_ref[...] = jnp.zeros_like(l_ref)
        acc_ref[...] = jnp.zeros_like(acc_ref)

    def step(masked):
        for hh in range(heads):
            sl = slice(hh * HEAD_DIM, (hh + 1) * HEAD_DIM)
            s = _dot_nt(q_ref[0, :, sl], k_ref[0, :, sl]) * scale
            s = s + ccol_ref[0, hh] - crow_ref[0, hh]
            if masked:
                row = lax.broadcasted_iota(jnp.int32, (tq, tk), 0)
                col = lax.broadcasted_iota(jnp.int32, (tq, tk), 1)
                s = jnp.where(col <= row, s, NEG_INF)
            m_ref[hh], l_ref[hh], acc_ref[hh] = _softmax_step(
                s, v_ref[0, :, sl], m_ref[hh], l_ref[hh], acc_ref[hh])

    @pl.when(ki < qi)
    def _():
        step(False)

    @pl.when(ki == qi)
    def _():
        step(True)
        for hh in range(heads):
            o_ref[0, :, hh * HEAD_DIM:(hh + 1) * HEAD_DIM] = acc_ref[hh] / l_ref[hh]


def _attn_prompt(q, kb, vb, c_t, *, tq, heads):
    b, s, da = q.shape
    nh = da // HEAD_DIM
    tq = min(tq, s)
    heads = min(heads, nh)
    nq = s // tq
    pairs = [(i, j) for i in range(nq) for j in range(i + 1)]
    qi_tab = jnp.asarray([p[0] for p in pairs], jnp.int32)
    ki_tab = jnp.asarray([p[1] for p in pairs], jnp.int32)
    w = heads * HEAD_DIM
    grid_spec = pltpu.PrefetchScalarGridSpec(
        num_scalar_prefetch=2,
        grid=(b, nh // heads, len(pairs)),
        in_specs=[
            pl.BlockSpec((1, tq, w), lambda bi, hg, t, qt, kt: (bi, qt[t], hg)),
            pl.BlockSpec((1, tq, w), lambda bi, hg, t, qt, kt: (bi, kt[t], hg)),
            pl.BlockSpec((1, tq, w), lambda bi, hg, t, qt, kt: (bi, kt[t], hg)),
            pl.BlockSpec((1, heads, tq, 1), lambda bi, hg, t, qt, kt: (bi, hg, qt[t], 0)),
            pl.BlockSpec((1, heads, 1, tq), lambda bi, hg, t, qt, kt: (bi, hg, 0, kt[t])),
        ],
        out_specs=pl.BlockSpec((1, tq, w), lambda bi, hg, t, qt, kt: (bi, qt[t], hg)),
        scratch_shapes=[
            pltpu.VMEM((heads, tq, 1), F32),
            pltpu.VMEM((heads, tq, 1), F32),
            pltpu.VMEM((heads, tq, HEAD_DIM), F32),
        ],
    )
    return pl.pallas_call(
        functools.partial(_attn_prompt_kernel, heads=heads, scale=HEAD_DIM ** -0.5),
        grid_spec=grid_spec,
        out_shape=jax.ShapeDtypeStruct((b, s, da), F32),
        compiler_params=_cparams(("parallel", "parallel", "arbitrary")),
        name="attn_prompt",
    )(qi_tab, ki_tab, q, kb, vb, c_t.reshape(b, nh, s, 1), c_t.reshape(b, nh, 1, s))


def _attn_sample_kernel(q_ref, ck_ref, cv_ref, kn_ref, vn_ref, crow_ref, ccol_ref, o_ref, *, scale):
    t = q_ref.shape[1]
    past = ck_ref.shape[1]
    nh = q_ref.shape[2] // HEAD_DIM
    row = lax.broadcasted_iota(jnp.int32, (t, t), 0)
    col = lax.broadcasted_iota(jnp.int32, (t, t), 1)
    for hh in range(nh):
        sl = slice(hh * HEAD_DIM, (hh + 1) * HEAD_DIM)
        qh = q_ref[0, :, sl]
        cq = ccol_ref[0, :, hh:hh + 1]
        ck = crow_ref[0, hh:hh + 1, :]
        s1 = _dot_nt(qh, ck_ref[0, :, sl].astype(BF16)) * scale + cq - ck[:, :past]
        s2 = _dot_nt(qh, kn_ref[0, :, sl]) * scale + cq - ck[:, past:]
        s2 = jnp.where(col <= row, s2, NEG_INF)
        m = jnp.maximum(jnp.max(s1, axis=-1, keepdims=True), jnp.max(s2, axis=-1, keepdims=True))
        p1 = jnp.exp(s1 - m)
        p2 = jnp.exp(s2 - m)
        l = jnp.sum(p1, axis=-1, keepdims=True) + jnp.sum(p2, axis=-1, keepdims=True)
        o = _dot(p1.astype(BF16), cv_ref[0, :, sl].astype(BF16)) + _dot(p2.astype(BF16), vn_ref[0, :, sl])
        o_ref[0, :, sl] = o / l


def _attn_sample(q, cache_k, cache_v, kb, vb, c_t, c_col):
    b, t, da = q.shape
    past = cache_k.shape[1]
    nh = da // HEAD_DIM
    blk = lambda *shape: pl.BlockSpec((1,) + shape, lambda i: (i, 0, 0))
    return pl.pallas_call(
        functools.partial(_attn_sample_kernel, scale=HEAD_DIM ** -0.5),
        grid=(b,),
        in_specs=[blk(t, da), blk(past, da), blk(past, da), blk(t, da), blk(t, da),
                  blk(nh, past + t), blk(t, nh)],
        out_specs=blk(t, da),
        out_shape=jax.ShapeDtypeStruct((b, t, da), F32),
        compiler_params=_cparams(("parallel",)),
        name="attn_sample",
    )(q, cache_k, cache_v, kb, vb, c_t, c_col)


def _cexp(mag_log, ang):
    mag = jnp.exp(mag_log)
    return mag * jnp.cos(ang), mag * jnp.sin(ang)


def _cmul(ar, ai, br, bi):
    return ar * br - ai * bi, ar * bi + ai * br


def _gelu_tanh(x):
    return 0.5 * x * (1.0 + jnp.tanh(math.sqrt(2.0 / math.pi) * (x + 0.044715 * (x * x * x))))


def _split2(x):
    hi = x.astype(BF16)
    lo = (x - hi.astype(F32)).astype(BF16)
    return hi, lo


def _ssm_kernel(up_ref, us_ref, x0r_ref, x0i_ref, lrow_re_ref, lrow_im_ref, lcol_re_ref, lcol_im_ref,
                ldt_ref, l8_re_ref, l8_im_ref, ldt8_ref, bt_re_ref, bt_im_ref, ct_re_ref, ct_im_ref, d_ref,
                yp_ref, ys_ref, lastp_re_ref, lastp_im_ref, lasts_re_ref, lasts_im_ref,
                wr_ref, wi_ref, xr_ref, xi_ref, vre_ref, vim_ref, *, streams_p):
    j = pl.program_id(1)
    gb = GROUPS_PER_STEP
    rp = up_ref.shape[1]
    p = SSM_STATE
    dt = jnp.exp(ldt_ref[0])

    a_row = lrow_re_ref[0] * dt
    w_row = lrow_im_ref[0] * dt
    s_idx = (lax.broadcasted_iota(jnp.int32, (LANES, p), 0) // SSM_GROUP).astype(F32)
    en_re, en_im = _cexp(-(s_idx + 1.0) * a_row, -(s_idx + 1.0) * w_row)
    l8_re, l8_im = _cexp(CHUNK_T * a_row, CHUNK_T * w_row)
    lb_re, lb_im = _cexp(a_row, w_row)
    lr, li = lrow_re_ref[0], lrow_im_ref[0]
    inv = 1.0 / (lr * lr + li * li)
    cf_re, cf_im = _cmul(lb_re - 1.0, lb_im, lr * inv, -li * inv)
    bb_re, bb_im = _cmul(cf_re, cf_im, bt_re_ref[0], bt_im_ref[0])
    f_re, f_im = _cmul(en_re, en_im, bb_re, bb_im)
    e7_re, e7_im = _cmul(en_re, en_im, l8_re, l8_im)
    w_re, w_im = _cmul(e7_re, e7_im, bb_re, bb_im)
    w_re, w_im = w_re.astype(BF16), w_im.astype(BF16)

    a_col = lcol_re_ref[0] * dt
    w_col = lcol_im_ref[0] * dt
    t_idx = (lax.broadcasted_iota(jnp.int32, (p, LANES), 1) // SSM_GROUP).astype(F32)
    et_re, et_im = _cexp((t_idx + 1.0) * a_col, (t_idx + 1.0) * w_col)
    g_re, g_im = _cmul(et_re, et_im, ct_re_ref[0], ct_im_ref[0])
    v_re, v_im = g_re.astype(BF16), (-g_im).astype(BF16)
    vre_ref[j] = v_re
    vim_ref[j] = v_im

    fs = jnp.concatenate([f_re, -f_im], axis=1)
    hs = jnp.concatenate([g_re, g_im], axis=0)
    fh, fl = _split2(fs)
    hh, hl = _split2(hs)
    tmat = _dot(fh, hh) + _dot(fh, hl) + _dot(fl, hh)
    srow = lax.broadcasted_iota(jnp.int32, (LANES, LANES), 0) // SSM_GROUP
    tcol = lax.broadcasted_iota(jnp.int32, (LANES, LANES), 1) // SSM_GROUP
    tmat = jnp.where(srow <= tcol, tmat, 0.0).astype(BF16)

    d_row = d_ref[0]

    u = up_ref[0]
    ub = u.astype(BF16)
    yp_ref[j] = _dot(ub, tmat) + d_row * u
    base = pl.multiple_of(j * rp, 8)
    wr_ref[pl.ds(base, rp), :] = _dot(ub, w_re)
    wi_ref[pl.ds(base, rp), :] = _dot(ub, w_im)

    us = us_ref[0]
    usb = us.astype(BF16)
    ws_re = _dot(usb, w_re)
    ws_im = _dot(usb, w_im)
    nb = x0r_ref.shape[1]
    sr, si = x0r_ref[0], x0i_ref[0]
    xs_re, xs_im = [], []
    for c in range(us.shape[0] // nb):
        xs_re.append(sr)
        xs_im.append(si)
        nr, ni = _cmul(l8_re, l8_im, sr, si)
        sr = nr + ws_re[c * nb:(c + 1) * nb]
        si = ni + ws_im[c * nb:(c + 1) * nb]
    lasts_re_ref[0] = sr
    lasts_im_ref[0] = si
    xs_re = jnp.concatenate(xs_re, axis=0).astype(BF16)
    xs_im = jnp.concatenate(xs_im, axis=0).astype(BF16)
    ys_ref[0] = _gelu_tanh(_dot(usb, tmat) + _dot(xs_re, v_re) + _dot(xs_im, v_im) + d_row * us)

    @pl.when(j == gb - 1)
    def _():
        dt8 = jnp.exp(ldt8_ref[0])
        a8_re, a8_im = _cexp(CHUNK_T * l8_re_ref[0] * dt8, CHUNK_T * l8_im_ref[0] * dt8)
        nc = rp // streams_p

        def body(c, carry):
            new = []
            for b in range(streams_p):
                sr, si = carry[2 * b], carry[2 * b + 1]
                r = b * nc + c
                xr_ref[pl.ds(r, gb, stride=rp), :] = sr
                xi_ref[pl.ds(r, gb, stride=rp), :] = si
                nr, ni = _cmul(a8_re, a8_im, sr, si)
                new.append(nr + wr_ref[pl.ds(r, gb, stride=rp), :])
                new.append(ni + wi_ref[pl.ds(r, gb, stride=rp), :])
            return tuple(new)

        zero = jnp.zeros((gb, p), F32)
        fin = lax.fori_loop(0, nc, body, (zero,) * (2 * streams_p))
        for b in range(streams_p):
            lastp_re_ref[b] = fin[2 * b]
            lastp_im_ref[b] = fin[2 * b + 1]

        for g in range(gb):
            xr = xr_ref[g * rp:(g + 1) * rp, :].astype(BF16)
            xi = xi_ref[g * rp:(g + 1) * rp, :].astype(BF16)
            yp_ref[g] = _gelu_tanh(yp_ref[g] + _dot(xr, vre_ref[g]) + _dot(xi, vim_ref[g]))


def _ssm(up, us, x0_re, x0_im, lam_re, lam_im, log_dt, bt_re, bt_im, ct_re, ct_im, d_t, *, streams_p):
    g, rp, _ = up.shape
    rs = us.shape[1]
    nb = x0_re.shape[1]
    p = SSM_STATE
    gb = GROUPS_PER_STEP
    ngb = g // gb
    per_group = lambda *shape: pl.BlockSpec((1,) + shape, lambda i, j: (i * gb + j, 0, 0))
    per_block = lambda *shape: pl.BlockSpec((1,) + shape, lambda i, j: (i, 0, 0))
    return pl.pallas_call(
        functools.partial(_ssm_kernel, streams_p=streams_p),
        grid=(ngb, gb),
        in_specs=[
            per_group(rp, LANES), per_group(rs, LANES), per_group(nb, p), per_group(nb, p),
            per_group(1, p), per_group(1, p), per_group(p, 1), per_group(p, 1), per_group(1, 1),
            per_block(gb, p), per_block(gb, p), per_block(gb, 1),
            per_group(LANES, p), per_group(LANES, p), per_group(p, LANES), per_group(p, LANES),
            per_group(1, LANES),
        ],
        out_specs=[
            pl.BlockSpec((gb, rp, LANES), lambda i, j: (i, 0, 0)),
            per_group(rs, LANES),
            pl.BlockSpec((streams_p, gb, p), lambda i, j: (0, i, 0)),
            pl.BlockSpec((streams_p, gb, p), lambda i, j: (0, i, 0)),
            per_group(nb, p), per_group(nb, p),
        ],
        out_shape=[
            jax.ShapeDtypeStruct((g, rp, LANES), F32),
            jax.ShapeDtypeStruct((g, rs, LANES), F32),
            jax.ShapeDtypeStruct((streams_p, g, p), F32),
            jax.ShapeDtypeStruct((streams_p, g, p), F32),
            jax.ShapeDtypeStruct((g, nb, p), F32),
            jax.ShapeDtypeStruct((g, nb, p), F32),
        ],
        scratch_shapes=[
            pltpu.VMEM((gb * rp, p), F32), pltpu.VMEM((gb * rp, p), F32),
            pltpu.VMEM((gb * rp, p), F32), pltpu.VMEM((gb * rp, p), F32),
            pltpu.VMEM((gb, p, LANES), BF16), pltpu.VMEM((gb, p, LANES), BF16),
        ],
        compiler_params=_cparams(("parallel", "arbitrary")),
        name="ssm",
    )(up, us, x0_re, x0_im,
      lam_re.reshape(g, 1, p), lam_im.reshape(g, 1, p), lam_re.reshape(g, p, 1), lam_im.reshape(g, p, 1),
      log_dt.reshape(g, 1, 1),
      lam_re.reshape(ngb, gb, p), lam_im.reshape(ngb, gb, p), log_dt.reshape(ngb, gb, 1),
      bt_re, bt_im, ct_re, ct_im, d_t)


def _glu_kernel(y_ref, w_ref, o_ref):
    y = y_ref[...]
    o_ref[...] = y * jax.nn.sigmoid(_dot(y.astype(BF16), w_ref[...]))


def _glu(y, w, *, tm):
    t, d = y.shape
    tm = min(tm, t)
    return pl.pallas_call(
        _glu_kernel,
        grid=(t // tm,),
        in_specs=[pl.BlockSpec((tm, d), lambda i: (i, 0)), pl.BlockSpec((d, d), lambda i: (0, 0))],
        out_specs=pl.BlockSpec((tm, d), lambda i: (i, 0)),
        out_shape=jax.ShapeDtypeStruct((t, d), F32),
        compiler_params=_cparams(("parallel",)),
        name="glu",
    )(y, w)


def _merge_kernel(o_ref, y_ref, x_ref, ga_ref, gs_ref, w_ref, out_ref, a_ref):
    da = o_ref.shape[1]

    @pl.when(pl.program_id(1) == 0)
    def _():
        a_ref[:, :da] = _rms(o_ref[...], ga_ref[...]).astype(BF16)
        a_ref[:, da:] = _rms(y_ref[...], gs_ref[...]).astype(BF16)

    out_ref[...] = x_ref[...] + _dot(a_ref[...], w_ref[...])


def _merge(o, y, x, g_attn, g_ssm, w_out, *, tm, tn):
    t, da = o.shape
    ds = y.shape[1]
    d = x.shape[1]
    tm, tn = min(tm, t), min(tn, d)
    return pl.pallas_call(
        _merge_kernel,
        grid=(t // tm, d // tn),
        in_specs=[
            pl.BlockSpec((tm, da), lambda i, n: (i, 0)),
            pl.BlockSpec((tm, ds), lambda i, n: (i, 0)),
            pl.BlockSpec((tm, tn), lambda i, n: (i, n)),
            pl.BlockSpec((1, da), lambda i, n: (0, 0)),
            pl.BlockSpec((1, ds), lambda i, n: (0, 0)),
            pl.BlockSpec((da + ds, tn), lambda i, n: (0, n)),
        ],
        out_specs=pl.BlockSpec((tm, tn), lambda i, n: (i, n)),
        out_shape=jax.ShapeDtypeStruct((t, d), F32),
        scratch_shapes=[pltpu.VMEM((tm, da + ds), BF16)],
        compiler_params=_cparams(("parallel", "arbitrary")),
        name="merge",
    )(o, y, x, g_attn.reshape(1, da), g_ssm.reshape(1, ds), w_out)


def _to_chunks(u, groups, stream_major):
    b, s, _ = u.shape
    u = u.reshape(b, s // CHUNK_T, CHUNK_T, groups, SSM_GROUP)
    perm = (3, 0, 1, 2, 4) if stream_major else (3, 1, 0, 2, 4)
    return u.transpose(perm).reshape(groups, b * (s // CHUNK_T), LANES)


def _from_chunks(y, b, s, stream_major):
    groups = y.shape[0]
    nc = s // CHUNK_T
    if stream_major:
        y = y.reshape(groups, b, nc, CHUNK_T, SSM_GROUP).transpose(1, 2, 3, 0, 4)
    else:
        y = y.reshape(groups, nc, b, CHUNK_T, SSM_GROUP).transpose(2, 1, 3, 0, 4)
    return y.reshape(b, s, groups * SSM_GROUP)


def kernel(x_prompt, x_sample, cache_k, cache_v, cache_logf, state_ssm_re, state_ssm_im, norm_ffn1, w1_a, w3_a, w2_a, norm_mix, w_in, b_f, q_norm, k_norm, lam_re, lam_im, log_dt, b_re, b_im, c_re, c_im, d_skip, w_glu, out_norm_attn, out_norm_ssm, w_out, norm_ffn2, w1_b, w3_b, w2_b):
    depth = norm_ffn1.shape[0]
    assert depth == 1
    l = 0
    bp, sp, d = x_prompt.shape
    bs, ts, _ = x_sample.shape
    past = cache_k.shape[2]
    nh = b_f.shape[1]
    da = nh * HEAD_DIM
    groups = lam_re.shape[1]
    dssm = groups * SSM_GROUP
    p = SSM_STATE

    bf = lambda w: w.astype(BF16)
    w_main = bf(jnp.concatenate([w_in[l][:, :3 * da], w_in[l][:, 3 * da + nh:]], axis=1))
    w_f = bf(w_in[l][:, 3 * da:3 * da + nh])

    xp = x_prompt.reshape(bp * sp, d)
    xs = x_sample.reshape(bs * ts, d)

    ffn_a = functools.partial(_ffn, g=norm_ffn1[l], w1=bf(w1_a[l]), w3=bf(w3_a[l]), w2=bf(w2_a[l]), tm=512, tf=256)
    xp = ffn_a(xp)
    xs = ffn_a(xs)

    proj = functools.partial(_inproj, g=norm_mix[l], w_main=w_main, w_f=w_f, b_f=b_f[l],
                             q_norm=q_norm[l], k_norm=k_norm[l], tm=512, tn=512)
    qp, kp, kbp, vp, vbp, up, lfp = proj(xp)
    qs, ks, kbs, vs, vbs, us, lfs = proj(xs)

    lfp_t = lfp.reshape(bp, sp, nh).transpose(0, 2, 1).reshape(bp * nh, sp)
    cp_t = _cumsum_rows(lfp_t, bs=512).reshape(bp, nh, sp)
    o_p = _attn_prompt(qp.reshape(bp, sp, da), kbp.reshape(bp, sp, da), vbp.reshape(bp, sp, da), cp_t,
                       tq=512, heads=4)

    lfs_all = jnp.concatenate([cache_logf[l].astype(F32), lfs.reshape(bs, ts, nh)], axis=1)
    lfs_t = lfs_all.transpose(0, 2, 1).reshape(bs * nh, past + ts)
    cs_t = _cumsum_rows(lfs_t, bs=past + ts).reshape(bs, nh, past + ts)
    cs_col = cs_t[:, :, past:].transpose(0, 2, 1)
    o_s = _attn_sample(qs.reshape(bs, ts, da), cache_k[l].reshape(bs, past, da), cache_v[l].reshape(bs, past, da),
                       kbs.reshape(bs, ts, da), vbs.reshape(bs, ts, da), cs_t, cs_col)

    bt = lambda b: jnp.tile(b.transpose(0, 2, 1), (1, CHUNK_T, 1))
    ct = lambda c: jnp.tile(c.transpose(0, 2, 1), (1, 1, CHUNK_T))
    d_t = jnp.tile(d_skip[l].reshape(groups, 1, SSM_GROUP), (1, 1, CHUNK_T))
    yp_c, ys_c, lp_re, lp_im, ls_re, ls_im = _ssm(
        _to_chunks(up.reshape(bp, sp, dssm), groups, True),
        _to_chunks(us.reshape(bs, ts, dssm), groups, False),
        state_ssm_re[l].astype(F32).transpose(1, 0, 2), state_ssm_im[l].astype(F32).transpose(1, 0, 2),
        lam_re[l], lam_im[l], log_dt[l], bt(b_re[l]), bt(b_im[l]), ct(c_re[l]), ct(c_im[l]), d_t,
        streams_p=bp)
    yp = _from_chunks(yp_c, bp, sp, True).reshape(bp * sp, dssm)
    ys = _from_chunks(ys_c, bs, ts, False).reshape(bs * ts, dssm)

    glu = functools.partial(_glu, w=bf(w_glu[l]), tm=512)
    merge = functools.partial(_merge, g_attn=out_norm_attn[l], g_ssm=out_norm_ssm[l], w_out=bf(w_out[l]),
                              tm=512, tn=512)
    xp = merge(o_p.reshape(bp * sp, da), glu(yp), xp)
    xs = merge(o_s.reshape(bs * ts, da), glu(ys), xs)

    ffn_b = functools.partial(_ffn, g=norm_ffn2[l], w1=bf(w1_b[l]), w3=bf(w3_b[l]), w2=bf(w2_b[l]), tm=512, tf=256)
    xp = ffn_b(xp)
    xs = ffn_b(xs)

    return (xp.reshape(bp, sp, d), xs.reshape(bs, ts, d),
            kp.reshape(1, bp, sp, nh, HEAD_DIM), vp.reshape(1, bp, sp, nh, HEAD_DIM), lfp.reshape(1, bp, sp, nh),
            lp_re[None], lp_im[None],
            ks.reshape(1, bs, ts, nh, HEAD_DIM), vs.reshape(1, bs, ts, nh, HEAD_DIM), lfs.reshape(1, bs, ts, nh),
            ls_re.transpose(1, 0, 2)[None], ls_im.transpose(1, 0, 2)[None])
```

```python
import functools
import math

import jax
import jax.numpy as jnp
import numpy as np
from jax import lax
from jax.experimental import pallas as pl
from jax.experimental.pallas import tpu as pltpu

F32 = jnp.float32
BF16 = jnp.bfloat16

EPS = 1e-6
NEG_INF = -1e30
LOG2E = math.log2(math.e)
HEAD_DIM = 128
SSM_GROUP = 16
SSM_STATE = 64
LANES = 128
CHUNK_T = LANES // SSM_GROUP
GROUPS_PER_STEP = 8
VMEM_LIMIT = 56 * 1024 * 1024


def _cparams(sem):
    return pltpu.CompilerParams(dimension_semantics=sem, vmem_limit_bytes=VMEM_LIMIT)


def _rms(x, g):
    ms = jnp.mean(x * x, axis=-1, keepdims=True)
    return x * lax.rsqrt(ms + EPS) * g


def _dot(a, b):
    return jnp.dot(a, b, preferred_element_type=F32)


def _dot_nt(a, b):
    return lax.dot_general(a, b, (((1,), (1,)), ((), ())), preferred_element_type=F32)


def _ffn_up_kernel(x_ref, g_ref, w1_ref, w3_ref, act_ref, h_ref):
    @pl.when(pl.program_id(1) == 0)
    def _():
        h_ref[...] = _rms(x_ref[...], g_ref[...]).astype(BF16)

    h = h_ref[...]
    a = _dot(h, w1_ref[...])
    b = _dot(h, w3_ref[...])
    act_ref[...] = (0.5 * a * jax.nn.sigmoid(a) * b).astype(BF16)


def _ffn_down_kernel(act_ref, w2_ref, x_ref, o_ref):
    o_ref[...] = x_ref[...] + _dot(act_ref[...], w2_ref[...])


def _ffn(x, g, w1, w3, w2, *, tm, tf, tn):
    t, d = x.shape
    f = w1.shape[1]
    tm, tf, tn = min(tm, t), min(tf, f), min(tn, d)
    once = pl.Buffered(1)
    act = pl.pallas_call(
        _ffn_up_kernel,
        grid=(t // tm, f // tf),
        in_specs=[
            pl.BlockSpec((tm, d), lambda i, j: (i, 0), pipeline_mode=once),
            pl.BlockSpec((1, d), lambda i, j: (0, 0)),
            pl.BlockSpec((d, tf), lambda i, j: (0, j)),
            pl.BlockSpec((d, tf), lambda i, j: (0, j)),
        ],
        out_specs=pl.BlockSpec((tm, tf), lambda i, j: (i, j)),
        out_shape=jax.ShapeDtypeStruct((t, f), BF16),
        scratch_shapes=[pltpu.VMEM((tm, d), BF16)],
        compiler_params=_cparams(("parallel", "arbitrary")),
        name="ffn_up",
    )(x, g.reshape(1, d), w1, w3)
    return pl.pallas_call(
        _ffn_down_kernel,
        grid=(t // tm, d // tn),
        in_specs=[
            pl.BlockSpec((tm, f), lambda i, n: (i, 0), pipeline_mode=once),
            pl.BlockSpec((f, tn), lambda i, n: (0, n)),
            pl.BlockSpec((tm, tn), lambda i, n: (i, n)),
        ],
        out_specs=pl.BlockSpec((tm, tn), lambda i, n: (i, n)),
        out_shape=jax.ShapeDtypeStruct((t, d), F32),
        compiler_params=_cparams(("parallel", "arbitrary")),
        name="ffn_down",
    )(act, w2, x)


def _inproj_kernel(x_ref, g_ref, w_ref, wf_ref, bf_ref, qn_ref, kn_ref,
                   q_ref, k_ref, kb_ref, v_ref, vb_ref, u_ref, lf_ref, h_ref, *, tiles, heads):
    n = pl.program_id(1)

    @pl.when(n == 0)
    def _():
        h = _rms(x_ref[...], g_ref[...]).astype(BF16)
        h_ref[...] = h
        zf = _dot(h, wf_ref[...]) + bf_ref[...]
        lf_ref[...] = jnp.minimum(zf, 0.0) - jnp.log1p(jnp.exp(-jnp.abs(zf)))

    z = _dot(h_ref[...], w_ref[...])
    sec = n // tiles

    def head_norm(gain):
        outs = []
        for hh in range(heads):
            outs.append(_rms(z[:, hh * HEAD_DIM:(hh + 1) * HEAD_DIM], gain))
        return jnp.concatenate(outs, axis=-1) if heads > 1 else outs[0]

    @pl.when(sec == 0)
    def _():
        q_ref[...] = head_norm(qn_ref[...]).astype(BF16)

    @pl.when(sec == 1)
    def _():
        kk = head_norm(kn_ref[...])
        k_ref[...] = kk
        kb_ref[...] = kk.astype(BF16)

    @pl.when(sec == 2)
    def _():
        v_ref[...] = z
        vb_ref[...] = z.astype(BF16)

    @pl.when(sec == 3)
    def _():
        u_ref[...] = z


def _inproj(x, g, w_main, w_f, b_f, q_norm, k_norm, *, tm, tn):
    t, d = x.shape
    da = w_main.shape[1] // 4
    nh = w_f.shape[1]
    tm, tn = min(tm, t), min(tn, da)
    tiles = da // tn

    def sec_map(s):
        return lambda i, n: (i, jnp.clip(n - s * tiles, 0, tiles - 1))

    row = lambda i, n: (0, 0)
    f32_out = jax.ShapeDtypeStruct((t, da), F32)
    bf_out = jax.ShapeDtypeStruct((t, da), BF16)
    return pl.pallas_call(
        functools.partial(_inproj_kernel, tiles=tiles, heads=tn // HEAD_DIM),
        grid=(t // tm, 4 * tiles),
        in_specs=[
            pl.BlockSpec((tm, d), lambda i, n: (i, 0), pipeline_mode=pl.Buffered(1)),
            pl.BlockSpec((1, d), row),
            pl.BlockSpec((d, tn), lambda i, n: (0, n)),
            pl.BlockSpec((d, nh), row),
            pl.BlockSpec((1, nh), row),
            pl.BlockSpec((1, HEAD_DIM), row),
            pl.BlockSpec((1, HEAD_DIM), row),
        ],
        out_specs=[
            pl.BlockSpec((tm, tn), sec_map(0)),
            pl.BlockSpec((tm, tn), sec_map(1)),
            pl.BlockSpec((tm, tn), sec_map(1)),
            pl.BlockSpec((tm, tn), sec_map(2)),
            pl.BlockSpec((tm, tn), sec_map(2)),
            pl.BlockSpec((tm, tn), sec_map(3)),
            pl.BlockSpec((tm, nh), lambda i, n: (i, 0)),
        ],
        out_shape=[bf_out, f32_out, bf_out, f32_out, bf_out, f32_out,
                   jax.ShapeDtypeStruct((t, nh), F32)],
        scratch_shapes=[pltpu.VMEM((tm, d), BF16)],
        compiler_params=_cparams(("parallel", "arbitrary")),
        name="inproj",
    )(x, g.reshape(1, d), w_main, w_f, b_f.reshape(1, nh),
      q_norm.reshape(1, HEAD_DIM), k_norm.reshape(1, HEAD_DIM))


def _split3(x):
    hi = x.astype(BF16)
    r1 = x - hi.astype(F32)
    mid = r1.astype(BF16)
    lo = (r1 - mid.astype(F32)).astype(BF16)
    return hi, mid, lo


def _cumsum_kernel(x_ref, o_ref, carry_ref):
    @pl.when(pl.program_id(0) == 0)
    def _():
        carry_ref[...] = jnp.zeros_like(carry_ref)

    x = x_ref[...]
    bs = x.shape[1]
    row = lax.broadcasted_iota(jnp.int32, (bs, bs), 0)
    col = lax.broadcasted_iota(jnp.int32, (bs, bs), 1)
    tri = jnp.where(row <= col, 1.0, 0.0).astype(BF16)
    hi, mid, lo = _split3(x)
    c = _dot(hi, tri) + _dot(mid, tri) + _dot(lo, tri) + carry_ref[...]
    o_ref[...] = c
    carry_ref[...] = c[:, bs - 1:bs]


def _cumsum_rows(x, *, bs):
    r, s = x.shape
    bs = min(bs, s)
    return pl.pallas_call(
        _cumsum_kernel,
        grid=(s // bs,),
        in_specs=[pl.BlockSpec((r, bs), lambda j: (0, j))],
        out_specs=pl.BlockSpec((r, bs), lambda j: (0, j)),
        out_shape=jax.ShapeDtypeStruct((r, s), F32),
        scratch_shapes=[pltpu.VMEM((r, 1), F32)],
        compiler_params=_cparams(("arbitrary",)),
        name="cumsum",
    )(x)


def _lane_tile(x, width):
    return x if width == LANES else pltpu.repeat(x, width // LANES, axis=1)


def _attn_prompt_kernel(qi_tab, ki_tab, q_ref, k_ref, v_ref, ccol_ref, crow_ref, o_ref,
                        m_ref, l_ref, acc_ref, cq_ref, *, heads, scale):
    t = pl.program_id(2)
    qi, ki = qi_tab[t], ki_tab[t]
    tq, tk = q_ref.shape[1], k_ref.shape[1]

    @pl.when(ki == 0)
    def _():
        m_ref[...] = jnp.full_like(m_ref, NEG_INF)
        l_ref[...] = jnp.zeros_like(l_ref)
        acc_ref[...] = jnp.zeros_like(acc_ref)
        for hh in range(heads):
            cq_ref[hh] = jnp.broadcast_to(ccol_ref[0, hh] * LOG2E, (tq, LANES))

    def step(masked):
        for hh in range(heads):
            sl = slice(hh * HEAD_DIM, (hh + 1) * HEAD_DIM)
            s = _dot_nt(q_ref[0, :, sl], k_ref[0, :, sl]) * (scale * LOG2E)
            s = s + _lane_tile(cq_ref[hh], tk) - crow_ref[0, hh] * LOG2E
            if masked:
                row = lax.broadcasted_iota(jnp.int32, (tq, tk), 0)
                col = lax.broadcasted_iota(jnp.int32, (tq, tk), 1)
                s = jnp.where(col <= row, s, NEG_INF)
            m_prev = m_ref[hh]
            m_new = jnp.maximum(m_prev, jnp.max(s, axis=-1, keepdims=True))
            alpha = jnp.exp2(m_prev - m_new)
            p = jnp.exp2(s - _lane_tile(m_new, tk))
            l_ref[hh] = alpha * l_ref[hh] + jnp.sum(p, axis=-1, keepdims=True)
            acc_ref[hh] = alpha * acc_ref[hh] + _dot(p.astype(BF16), v_ref[0, :, sl])
            m_ref[hh] = m_new

    @pl.when(ki < qi)
    def _():
        step(False)

    @pl.when(ki == qi)
    def _():
        step(True)
        for hh in range(heads):
            o_ref[0, :, hh * HEAD_DIM:(hh + 1) * HEAD_DIM] = acc_ref[hh] / l_ref[hh]


def _attn_prompt(q, kb, vb, c_t, *, tq, heads):
    b, s, da = q.shape
    nh = da // HEAD_DIM
    tq = min(tq, s)
    heads = min(heads, nh)
    nq = s // tq
    pairs = [(i, j) for i in range(nq) for j in range(i + 1)]
    qi_tab = jnp.asarray([p[0] for p in pairs], jnp.int32)
    ki_tab = jnp.asarray([p[1] for p in pairs], jnp.int32)
    w = heads * HEAD_DIM
    grid_spec = pltpu.PrefetchScalarGridSpec(
        num_scalar_prefetch=2,
        grid=(b, nh // heads, len(pairs)),
        in_specs=[
            pl.BlockSpec((1, tq, w), lambda bi, hg, t, qt, kt: (bi, qt[t], hg)),
            pl.BlockSpec((1, tq, w), lambda bi, hg, t, qt, kt: (bi, kt[t], hg)),
            pl.BlockSpec((1, tq, w), lambda bi, hg, t, qt, kt: (bi, kt[t], hg)),
            pl.BlockSpec((1, heads, tq, 1), lambda bi, hg, t, qt, kt: (bi, hg, qt[t], 0)),
            pl.BlockSpec((1, heads, 1, tq), lambda bi, hg, t, qt, kt: (bi, hg, 0, kt[t])),
        ],
        out_specs=pl.BlockSpec((1, tq, w), lambda bi, hg, t, qt, kt: (bi, qt[t], hg)),
        scratch_shapes=[
            pltpu.VMEM((heads, tq, LANES), F32),
            pltpu.VMEM((heads, tq, LANES), F32),
            pltpu.VMEM((heads, tq, HEAD_DIM), F32),
            pltpu.VMEM((heads, tq, LANES), F32),
        ],
    )
    return pl.pallas_call(
        functools.partial(_attn_prompt_kernel, heads=heads, scale=HEAD_DIM ** -0.5),
        grid_spec=grid_spec,
        out_shape=jax.ShapeDtypeStruct((b, s, da), F32),
        compiler_params=_cparams(("parallel", "parallel", "arbitrary")),
        name="attn_prompt",
    )(qi_tab, ki_tab, q, kb, vb, c_t.reshape(b, nh, s, 1), c_t.reshape(b, nh, 1, s))


def _attn_sample_kernel(q_ref, ck_ref, cv_ref, kn_ref, vn_ref, crow_ref, ccol_ref, o_ref, *, scale):
    t = q_ref.shape[1]
    past = ck_ref.shape[1]
    nh = q_ref.shape[2] // HEAD_DIM
    row = lax.broadcasted_iota(jnp.int32, (t, t), 0)
    col = lax.broadcasted_iota(jnp.int32, (t, t), 1)
    for hh in range(nh):
        sl = slice(hh * HEAD_DIM, (hh + 1) * HEAD_DIM)
        qh = q_ref[0, :, sl]
        cq = ccol_ref[0, :, hh:hh + 1]
        ck = crow_ref[0, hh:hh + 1, :]
        s1 = _dot_nt(qh, ck_ref[0, :, sl].astype(BF16)) * scale + cq - ck[:, :past]
        s2 = _dot_nt(qh, kn_ref[0, :, sl]) * scale + cq - ck[:, past:]
        s2 = jnp.where(col <= row, s2, NEG_INF)
        m = jnp.maximum(jnp.max(s1, axis=-1, keepdims=True), jnp.max(s2, axis=-1, keepdims=True))
        p1 = jnp.exp(s1 - m)
        p2 = jnp.exp(s2 - m)
        l = jnp.sum(p1, axis=-1, keepdims=True) + jnp.sum(p2, axis=-1, keepdims=True)
        o = _dot(p1.astype(BF16), cv_ref[0, :, sl].astype(BF16)) + _dot(p2.astype(BF16), vn_ref[0, :, sl])
        o_ref[0, :, sl] = o / l


def _attn_sample(q, cache_k, cache_v, kb, vb, c_t, c_col):
    b, t, da = q.shape
    past = cache_k.shape[1]
    nh = da // HEAD_DIM
    blk = lambda *shape: pl.BlockSpec((1,) + shape, lambda i: (i, 0, 0))
    return pl.pallas_call(
        functools.partial(_attn_sample_kernel, scale=HEAD_DIM ** -0.5),
        grid=(b,),
        in_specs=[blk(t, da), blk(past, da), blk(past, da), blk(t, da), blk(t, da),
                  blk(nh, past + t), blk(t, nh)],
        out_specs=blk(t, da),
        out_shape=jax.ShapeDtypeStruct((b, t, da), F32),
        compiler_params=_cparams(("parallel",)),
        name="attn_sample",
    )(q, cache_k, cache_v, kb, vb, c_t, c_col)


def _cexp(mag_log, ang):
    mag = jnp.exp(mag_log)
    return mag * jnp.cos(ang), mag * jnp.sin(ang)


def _cmul(ar, ai, br, bi):
    return ar * br - ai * bi, ar * bi + ai * br


def _gelu_tanh(x):
    return 0.5 * x * (1.0 + jnp.tanh(math.sqrt(2.0 / math.pi) * (x + 0.044715 * (x * x * x))))


def _split2(x):
    hi = x.astype(BF16)
    lo = (x - hi.astype(F32)).astype(BF16)
    return hi, lo


def _lane_block(rows):
    return lax.broadcasted_iota(jnp.int32, (rows, LANES), 1) // SSM_GROUP


def _to_chunk_layout(src_ref, dst_ref):
    rows = dst_ref.shape[1]
    blk = _lane_block(rows)
    toks = [src_ref[pl.ds(t, rows, stride=CHUNK_T), :] for t in range(CHUNK_T)]
    for i in range(CHUNK_T):
        acc = pltpu.roll(toks[0], ((0 - i) % CHUNK_T) * SSM_GROUP, axis=1)
        for t in range(1, CHUNK_T):
            acc = jnp.where(blk == t, pltpu.roll(toks[t], ((t - i) % CHUNK_T) * SSM_GROUP, axis=1), acc)
        dst_ref[i] = acc


def _from_chunk_layout(src_ref, dst_ref):
    rows = src_ref.shape[1]
    blk = _lane_block(rows)
    for t in range(CHUNK_T):
        acc = pltpu.roll(src_ref[0], ((0 - t) % CHUNK_T) * SSM_GROUP, axis=1)
        for i in range(1, CHUNK_T):
            acc = jnp.where(blk == i, pltpu.roll(src_ref[i], ((i - t) % CHUNK_T) * SSM_GROUP, axis=1), acc)
        dst_ref[pl.ds(t, rows, stride=CHUNK_T), :] = acc


def _ssm_kernel(up_ref, us_ref, x0r_ref, x0i_ref, lrow_re_ref, lrow_im_ref, lcol_re_ref, lcol_im_ref,
                ldt_ref, l8_re_ref, l8_im_ref, ldt8_ref, bt_re_ref, bt_im_ref, ct_re_ref, ct_im_ref, d_ref,
                yp_ref, ys_ref, lastp_re_ref, lastp_im_ref, lasts_re_ref, lasts_im_ref,
                cp_ref, cs_ref, wr_ref, wi_ref, xr_ref, xi_ref, wsr_ref, wsi_ref, xsr_ref, xsi_ref,
                vre_ref, vim_ref, *, streams_p):
    j = pl.program_id(1)
    gb = GROUPS_PER_STEP
    rp = cp_ref.shape[1]
    p = SSM_STATE

    @pl.when(j == 0)
    def _():
        _to_chunk_layout(up_ref, cp_ref)
        _to_chunk_layout(us_ref, cs_ref)

    dt = jnp.exp(ldt_ref[0])

    a_row = lrow_re_ref[0] * dt
    w_row = lrow_im_ref[0] * dt
    s_idx = (lax.broadcasted_iota(jnp.int32, (LANES, p), 0) // SSM_GROUP).astype(F32)
    en_re, en_im = _cexp(-(s_idx + 1.0) * a_row, -(s_idx + 1.0) * w_row)
    l8_re, l8_im = _cexp(CHUNK_T * a_row, CHUNK_T * w_row)
    lb_re, lb_im = _cexp(a_row, w_row)
    lr, li = lrow_re_ref[0], lrow_im_ref[0]
    inv = 1.0 / (lr * lr + li * li)
    cf_re, cf_im = _cmul(lb_re - 1.0, lb_im, lr * inv, -li * inv)
    bb_re, bb_im = _cmul(cf_re, cf_im, bt_re_ref[0], bt_im_ref[0])
    f_re, f_im = _cmul(en_re, en_im, bb_re, bb_im)
    e7_re, e7_im = _cmul(en_re, en_im, l8_re, l8_im)
    w_re, w_im = _cmul(e7_re, e7_im, bb_re, bb_im)
    w_re, w_im = w_re.astype(BF16), w_im.astype(BF16)

    a_col = lcol_re_ref[0] * dt
    w_col = lcol_im_ref[0] * dt
    t_idx = (lax.broadcasted_iota(jnp.int32, (p, LANES), 1) // SSM_GROUP).astype(F32)
    et_re, et_im = _cexp((t_idx + 1.0) * a_col, (t_idx + 1.0) * w_col)
    g_re, g_im = _cmul(et_re, et_im, ct_re_ref[0], ct_im_ref[0])
    v_re, v_im = g_re.astype(BF16), (-g_im).astype(BF16)
    vre_ref[j] = v_re
    vim_ref[j] = v_im

    fs = jnp.concatenate([f_re, -f_im], axis=1)
    hs = jnp.concatenate([g_re, g_im], axis=0)
    fh, fl = _split2(fs)
    hh, hl = _split2(hs)
    tmat = _dot(fh, hh) + _dot(fh, hl) + _dot(fl, hh)
    srow = lax.broadcasted_iota(jnp.int32, (LANES, LANES), 0) // SSM_GROUP
    tcol = lax.broadcasted_iota(jnp.int32, (LANES, LANES), 1) // SSM_GROUP
    tmat = jnp.where(srow <= tcol, tmat, 0.0).astype(BF16)

    d_row = d_ref[0]

    u = cp_ref[j]
    ub = u.astype(BF16)
    cp_ref[j] = _dot(ub, tmat) + d_row * u
    base = pl.multiple_of(j * rp, 8)
    wr_ref[pl.ds(base, rp), :] = _dot(ub, w_re)
    wi_ref[pl.ds(base, rp), :] = _dot(ub, w_im)

    us = cs_ref[j]
    usb = us.astype(BF16)
    wsr_ref[...] = _dot(usb, w_re)
    wsi_ref[...] = _dot(usb, w_im)
    nb = x0r_ref.shape[1]
    ncs = us.shape[0] // nb
    sr, si = x0r_ref[0], x0i_ref[0]
    for c in range(ncs):
        rows = pl.ds(c, nb, stride=ncs)
        xsr_ref[rows, :] = sr
        xsi_ref[rows, :] = si
        nr, ni = _cmul(l8_re, l8_im, sr, si)
        sr = nr + wsr_ref[rows, :]
        si = ni + wsi_ref[rows, :]
    lasts_re_ref[0] = sr
    lasts_im_ref[0] = si
    xs_re = xsr_ref[...].astype(BF16)
    xs_im = xsi_ref[...].astype(BF16)
    cs_ref[j] = _gelu_tanh(_dot(usb, tmat) + _dot(xs_re, v_re) + _dot(xs_im, v_im) + d_row * us)

    @pl.when(j == gb - 1)
    def _():
        dt8 = jnp.exp(ldt8_ref[0])
        a8_re, a8_im = _cexp(CHUNK_T * l8_re_ref[0] * dt8, CHUNK_T * l8_im_ref[0] * dt8)
        nc = rp // streams_p

        def body(c, carry):
            new = []
            for b in range(streams_p):
                sr, si = carry[2 * b], carry[2 * b + 1]
                r = b * nc + c
                xr_ref[pl.ds(r, gb, stride=rp), :] = sr
                xi_ref[pl.ds(r, gb, stride=rp), :] = si
                nr, ni = _cmul(a8_re, a8_im, sr, si)
                new.append(nr + wr_ref[pl.ds(r, gb, stride=rp), :])
                new.append(ni + wi_ref[pl.ds(r, gb, stride=rp), :])
            return tuple(new)

        zero = jnp.zeros((gb, p), F32)
        fin = lax.fori_loop(0, nc, body, (zero,) * (2 * streams_p))
        for b in range(streams_p):
            lastp_re_ref[b] = fin[2 * b]
            lastp_im_ref[b] = fin[2 * b + 1]

        for g in range(gb):
            xr = xr_ref[g * rp:(g + 1) * rp, :].astype(BF16)
            xi = xi_ref[g * rp:(g + 1) * rp, :].astype(BF16)
            cp_ref[g] = _gelu_tanh(cp_ref[g] + _dot(xr, vre_ref[g]) + _dot(xi, vim_ref[g]))

        _from_chunk_layout(cp_ref, yp_ref)
        _from_chunk_layout(cs_ref, ys_ref)


def _ssm(up, us, x0_re, x0_im, lam_re, lam_im, log_dt, bt_re, bt_im, ct_re, ct_im, d_t, *, streams_p):
    tp, dssm = up.shape
    ts = us.shape[0]
    g = dssm // SSM_GROUP
    rp, rs = tp // CHUNK_T, ts // CHUNK_T
    nb = x0_re.shape[1]
    p = SSM_STATE
    gb = GROUPS_PER_STEP
    ngb = g // gb
    per_group = lambda *shape: pl.BlockSpec((1,) + shape, lambda i, j: (i * gb + j, 0, 0))
    per_block = lambda *shape: pl.BlockSpec((1,) + shape, lambda i, j: (i, 0, 0))
    tokens = lambda rows, **kw: pl.BlockSpec((rows, LANES), lambda i, j: (0, i), **kw)
    return pl.pallas_call(
        functools.partial(_ssm_kernel, streams_p=streams_p),
        grid=(ngb, gb),
        in_specs=[
            tokens(tp, pipeline_mode=pl.Buffered(1)), tokens(ts, pipeline_mode=pl.Buffered(1)),
            per_group(nb, p), per_group(nb, p),
            per_group(1, p), per_group(1, p), per_group(p, 1), per_group(p, 1), per_group(1, 1),
            per_block(gb, p), per_block(gb, p), per_block(gb, 1),
            per_group(LANES, p), per_group(LANES, p), per_group(p, LANES), per_group(p, LANES),
            per_group(1, LANES),
        ],
        out_specs=[
            tokens(tp), tokens(ts),
            pl.BlockSpec((streams_p, gb, p), lambda i, j: (0, i, 0)),
            pl.BlockSpec((streams_p, gb, p), lambda i, j: (0, i, 0)),
            per_group(nb, p), per_group(nb, p),
        ],
        out_shape=[
            jax.ShapeDtypeStruct((tp, dssm), F32),
            jax.ShapeDtypeStruct((ts, dssm), F32),
            jax.ShapeDtypeStruct((streams_p, g, p), F32),
            jax.ShapeDtypeStruct((streams_p, g, p), F32),
            jax.ShapeDtypeStruct((g, nb, p), F32),
            jax.ShapeDtypeStruct((g, nb, p), F32),
        ],
        scratch_shapes=[
            pltpu.VMEM((gb, rp, LANES), F32), pltpu.VMEM((gb, rs, LANES), F32),
            pltpu.VMEM((gb * rp, p), F32), pltpu.VMEM((gb * rp, p), F32),
            pltpu.VMEM((gb * rp, p), F32), pltpu.VMEM((gb * rp, p), F32),
            pltpu.VMEM((rs, p), F32), pltpu.VMEM((rs, p), F32),
            pltpu.VMEM((rs, p), F32), pltpu.VMEM((rs, p), F32),
            pltpu.VMEM((gb, p, LANES), BF16), pltpu.VMEM((gb, p, LANES), BF16),
        ],
        compiler_params=_cparams(("parallel", "arbitrary")),
        name="ssm",
    )(up, us, x0_re, x0_im,
      lam_re.reshape(g, 1, p), lam_im.reshape(g, 1, p), lam_re.reshape(g, p, 1), lam_im.reshape(g, p, 1),
      log_dt.reshape(g, 1, 1),
      lam_re.reshape(ngb, gb, p), lam_im.reshape(ngb, gb, p), log_dt.reshape(ngb, gb, 1),
      bt_re, bt_im, ct_re, ct_im, d_t)


def _glu_kernel(y_ref, w_ref, o_ref):
    y = y_ref[...]
    o_ref[...] = y * jax.nn.sigmoid(_dot(y.astype(BF16), w_ref[...]))


def _glu(y, w, *, tm):
    t, d = y.shape
    tm = min(tm, t)
    return pl.pallas_call(
        _glu_kernel,
        grid=(t // tm,),
        in_specs=[pl.BlockSpec((tm, d), lambda i: (i, 0)), pl.BlockSpec((d, d), lambda i: (0, 0))],
        out_specs=pl.BlockSpec((tm, d), lambda i: (i, 0)),
        out_shape=jax.ShapeDtypeStruct((t, d), F32),
        compiler_params=_cparams(("parallel",)),
        name="glu",
    )(y, w)


def _merge_kernel(o_ref, y_ref, x_ref, ga_ref, gs_ref, w_ref, out_ref, a_ref):
    da = o_ref.shape[1]

    @pl.when(pl.program_id(1) == 0)
    def _():
        a_ref[:, :da] = _rms(o_ref[...], ga_ref[...]).astype(BF16)
        a_ref[:, da:] = _rms(y_ref[...], gs_ref[...]).astype(BF16)

    out_ref[...] = x_ref[...] + _dot(a_ref[...], w_ref[...])


def _merge(o, y, x, g_attn, g_ssm, w_out, *, tm, tn):
    t, da = o.shape
    ds = y.shape[1]
    d = x.shape[1]
    tm, tn = min(tm, t), min(tn, d)
    return pl.pallas_call(
        _merge_kernel,
        grid=(t // tm, d // tn),
        in_specs=[
            pl.BlockSpec((tm, da), lambda i, n: (i, 0), pipeline_mode=pl.Buffered(1)),
            pl.BlockSpec((tm, ds), lambda i, n: (i, 0), pipeline_mode=pl.Buffered(1)),
            pl.BlockSpec((tm, tn), lambda i, n: (i, n)),
            pl.BlockSpec((1, da), lambda i, n: (0, 0)),
            pl.BlockSpec((1, ds), lambda i, n: (0, 0)),
            pl.BlockSpec((da + ds, tn), lambda i, n: (0, n)),
        ],
        out_specs=pl.BlockSpec((tm, tn), lambda i, n: (i, n)),
        out_shape=jax.ShapeDtypeStruct((t, d), F32),
        scratch_shapes=[pltpu.VMEM((tm, da + ds), BF16)],
        compiler_params=_cparams(("parallel", "arbitrary")),
        name="merge",
    )(o, y, x, g_attn.reshape(1, da), g_ssm.reshape(1, ds), w_out)


TILES = dict(
    ffn=dict(tm=1024, tf=256, tn=256),
    inproj=dict(tm=512, tn=1024),
    attn_prompt=dict(tq=512, heads=4),
    cumsum=dict(bs=512),
    glu=dict(tm=512),
    merge=dict(tm=512, tn=1024),
)


def kernel(x_prompt, x_sample, cache_k, cache_v, cache_logf, state_ssm_re, state_ssm_im, norm_ffn1, w1_a, w3_a, w2_a, norm_mix, w_in, b_f, q_norm, k_norm, lam_re, lam_im, log_dt, b_re, b_im, c_re, c_im, d_skip, w_glu, out_norm_attn, out_norm_ssm, w_out, norm_ffn2, w1_b, w3_b, w2_b):
    depth = norm_ffn1.shape[0]
    assert depth == 1
    l = 0
    bp, sp, d = x_prompt.shape
    bs, ts, _ = x_sample.shape
    past = cache_k.shape[2]
    nh = b_f.shape[1]
    da = nh * HEAD_DIM
    groups = lam_re.shape[1]
    dssm = groups * SSM_GROUP
    p = SSM_STATE

    bf = lambda w: w.astype(BF16)
    w_main = bf(jnp.concatenate([w_in[l][:, :3 * da], w_in[l][:, 3 * da + nh:]], axis=1))
    w_f = bf(w_in[l][:, 3 * da:3 * da + nh])

    xp = x_prompt.reshape(bp * sp, d)
    xs = x_sample.reshape(bs * ts, d)

    ffn_a = functools.partial(_ffn, g=norm_ffn1[l], w1=bf(w1_a[l]), w3=bf(w3_a[l]), w2=bf(w2_a[l]), **TILES["ffn"])
    xp = ffn_a(xp)
    xs = ffn_a(xs)

    proj = functools.partial(_inproj, g=norm_mix[l], w_main=w_main, w_f=w_f, b_f=b_f[l],
                             q_norm=q_norm[l], k_norm=k_norm[l], **TILES["inproj"])
    qp, kp, kbp, vp, vbp, up, lfp = proj(xp)
    qs, ks, kbs, vs, vbs, us, lfs = proj(xs)

    lfp_t = lfp.reshape(bp, sp, nh).transpose(0, 2, 1).reshape(bp * nh, sp)
    cp_t = _cumsum_rows(lfp_t, **TILES["cumsum"]).reshape(bp, nh, sp)
    o_p = _attn_prompt(qp.reshape(bp, sp, da), kbp.reshape(bp, sp, da), vbp.reshape(bp, sp, da), cp_t,
                       **TILES["attn_prompt"])

    lfs_all = jnp.concatenate([cache_logf[l].astype(F32), lfs.reshape(bs, ts, nh)], axis=1)
    lfs_t = lfs_all.transpose(0, 2, 1).reshape(bs * nh, past + ts)
    cs_t = _cumsum_rows(lfs_t, bs=past + ts).reshape(bs, nh, past + ts)
    cs_col = cs_t[:, :, past:].transpose(0, 2, 1)
    o_s = _attn_sample(qs.reshape(bs, ts, da), cache_k[l].reshape(bs, past, da), cache_v[l].reshape(bs, past, da),
                       kbs.reshape(bs, ts, da), vbs.reshape(bs, ts, da), cs_t, cs_col)

    bt = lambda b: jnp.tile(b.transpose(0, 2, 1), (1, CHUNK_T, 1))
    ct = lambda c: jnp.tile(c.transpose(0, 2, 1), (1, 1, CHUNK_T))
    d_t = jnp.tile(d_skip[l].reshape(groups, 1, SSM_GROUP), (1, 1, CHUNK_T))
    yp, ys, lp_re, lp_im, ls_re, ls_im = _ssm(
        up, us,
        state_ssm_re[l].astype(F32).transpose(1, 0, 2), state_ssm_im[l].astype(F32).transpose(1, 0, 2),
        lam_re[l], lam_im[l], log_dt[l], bt(b_re[l]), bt(b_im[l]), ct(c_re[l]), ct(c_im[l]), d_t,
        streams_p=bp)

    glu = functools.partial(_glu, w=bf(w_glu[l]), **TILES["glu"])
    merge = functools.partial(_merge, g_attn=out_norm_attn[l], g_ssm=out_norm_ssm[l], w_out=bf(w_out[l]),
                              **TILES["merge"])
    xp = merge(o_p.reshape(bp * sp, da), glu(yp), xp)
    xs = merge(o_s.reshape(bs * ts, da), glu(ys), xs)

    ffn_b = functools.partial(_ffn, g=norm_ffn2[l], w1=bf(w1_b[l]), w3=bf(w3_b[l]), w2=bf(w2_b[l]), **TILES["ffn"])
    xp = ffn_b(xp)
    xs = ffn_b(xs)

    return (xp.reshape(bp, sp, d), xs.reshape(bs, ts, d),
            kp.reshape(1, bp, sp, nh, HEAD_DIM), vp.reshape(1, bp, sp, nh, HEAD_DIM), lfp.reshape(1, bp, sp, nh),
            lp_re[None], lp_im[None],
            ks.reshape(1, bs, ts, nh, HEAD_DIM), vs.reshape(1, bs, ts, nh, HEAD_DIM), lfs.reshape(1, bs, ts, nh),
            ls_re.transpose(1, 0, 2)[None], ls_im.transpose(1, 0, 2)[None])
```

```python
import functools
import math

import jax
import jax.numpy as jnp
import numpy as np
from jax import lax
from jax.experimental import pallas as pl
from jax.experimental.pallas import tpu as pltpu

F32 = jnp.float32
BF16 = jnp.bfloat16

EPS = 1e-6
NEG_INF = -1e30
LOG2E = math.log2(math.e)
HEAD_DIM = 128
SSM_GROUP = 16
SSM_STATE = 64
LANES = 128
MXU_COLS = 256
CHUNK_T = LANES // SSM_GROUP
GROUPS_PER_STEP = 8
VMEM_LIMIT = 56 * 1024 * 1024


def _cparams(sem):
    return pltpu.CompilerParams(dimension_semantics=sem, vmem_limit_bytes=VMEM_LIMIT)


def _rms(x, g):
    ms = jnp.mean(x * x, axis=-1, keepdims=True)
    return x * lax.rsqrt(ms + EPS) * g


def _dot(a, b):
    return jnp.dot(a, b, preferred_element_type=F32)


def _dot_nt(a, b):
    return lax.dot_general(a, b, (((1,), (1,)), ((), ())), preferred_element_type=F32)


def _ffn_up_kernel(x_ref, g_ref, w1_ref, w3_ref, act_ref, h_ref):
    @pl.when(pl.program_id(1) == 0)
    def _():
        h_ref[...] = _rms(x_ref[...], g_ref[...]).astype(BF16)

    h = h_ref[...]
    a = _dot(h, w1_ref[...])
    b = _dot(h, w3_ref[...])
    act_ref[...] = (0.5 * a * jax.nn.sigmoid(a) * b).astype(BF16)


def _ffn_down_kernel(act_ref, w2_ref, x_ref, o_ref):
    o_ref[...] = x_ref[...] + _dot(act_ref[...], w2_ref[...])


def _ffn(x, g, w1, w3, w2, *, tm, tf, tm_down, tn):
    t, d = x.shape
    f = w1.shape[1]
    tm, tf, tn = min(tm, t), min(tf, f), min(tn, d)
    once = pl.Buffered(1)
    act = pl.pallas_call(
        _ffn_up_kernel,
        grid=(t // tm, f // tf),
        in_specs=[
            pl.BlockSpec((tm, d), lambda i, j: (i, 0), pipeline_mode=once),
            pl.BlockSpec((1, d), lambda i, j: (0, 0)),
            pl.BlockSpec((d, tf), lambda i, j: (0, j)),
            pl.BlockSpec((d, tf), lambda i, j: (0, j)),
        ],
        out_specs=pl.BlockSpec((tm, tf), lambda i, j: (i, j)),
        out_shape=jax.ShapeDtypeStruct((t, f), BF16),
        scratch_shapes=[pltpu.VMEM((tm, d), BF16)],
        compiler_params=_cparams(("parallel", "arbitrary")),
        name="ffn_up",
    )(x, g.reshape(1, d), w1, w3)
    tm = min(tm_down, t)
    return pl.pallas_call(
        _ffn_down_kernel,
        grid=(t // tm, d // tn),
        in_specs=[
            pl.BlockSpec((tm, f), lambda i, n: (i, 0)),
            pl.BlockSpec((f, tn), lambda i, n: (0, n)),
            pl.BlockSpec((tm, tn), lambda i, n: (i, n)),
        ],
        out_specs=pl.BlockSpec((tm, tn), lambda i, n: (i, n)),
        out_shape=jax.ShapeDtypeStruct((t, d), F32),
        compiler_params=_cparams(("parallel", "arbitrary")),
        name="ffn_down",
    )(act, w2, x)


def _inproj_kernel(x_ref, g_ref, w_ref, wf_ref, bf_ref, qn_ref, kn_ref,
                   q_ref, k_ref, kb_ref, v_ref, vb_ref, u_ref, lf_ref, h_ref, *, tiles):
    n = pl.program_id(1)

    @pl.when(n == 0)
    def _():
        h = _rms(x_ref[...], g_ref[...]).astype(BF16)
        h_ref[...] = h
        zf = _dot(h, wf_ref[...]) + bf_ref[...]
        lf_ref[...] = jnp.minimum(zf, 0.0) - jnp.log1p(jnp.exp(-jnp.abs(zf)))

    sec = n // tiles
    tn = w_ref.shape[1]
    cw = min(MXU_COLS, tn)
    hpc = cw // HEAD_DIM

    def chunks():
        for c in range(tn // cw):
            yield slice(c * cw, (c + 1) * cw), c * hpc, _dot(h_ref[...], w_ref[:, c * cw:(c + 1) * cw])

    def heads(z, h0):
        for hh in range(hpc):
            yield h0 + hh, z[:, hh * HEAD_DIM:(hh + 1) * HEAD_DIM]

    @pl.when(sec == 0)
    def _():
        for _, h0, z in chunks():
            for hd, zh in heads(z, h0):
                q_ref[:, hd * HEAD_DIM:(hd + 1) * HEAD_DIM] = _rms(zh, qn_ref[...]).astype(BF16)

    @pl.when(sec == 1)
    def _():
        for _, h0, z in chunks():
            for hd, zh in heads(z, h0):
                kk = _rms(zh, kn_ref[...])
                k_ref[:, hd, :] = kk
                kb_ref[:, hd * HEAD_DIM:(hd + 1) * HEAD_DIM] = kk.astype(BF16)

    @pl.when(sec == 2)
    def _():
        for _, h0, z in chunks():
            for hd, zh in heads(z, h0):
                v_ref[:, hd, :] = zh
                vb_ref[:, hd * HEAD_DIM:(hd + 1) * HEAD_DIM] = zh.astype(BF16)

    @pl.when(sec == 3)
    def _():
        for cols, _, z in chunks():
            u_ref[:, cols] = z


def _inproj(x, g, w_main, w_f, b_f, q_norm, k_norm, *, tm, tn):
    t, d = x.shape
    da = w_main.shape[1] // 4
    nh = w_f.shape[1]
    tm, tn = min(tm, t), min(tn, da)
    tiles = da // tn

    def sec_map(s):
        return lambda i, n: (i, jnp.clip(n - s * tiles, 0, tiles - 1))

    def head_map(s):
        return lambda i, n: (i, jnp.clip(n - s * tiles, 0, tiles - 1), 0)

    row = lambda i, n: (0, 0)
    f32_out = jax.ShapeDtypeStruct((t, da), F32)
    bf_out = jax.ShapeDtypeStruct((t, da), BF16)
    heads_out = jax.ShapeDtypeStruct((t, da // HEAD_DIM, HEAD_DIM), F32)
    heads_blk = (tm, tn // HEAD_DIM, HEAD_DIM)
    return pl.pallas_call(
        functools.partial(_inproj_kernel, tiles=tiles),
        grid=(t // tm, 4 * tiles),
        in_specs=[
            pl.BlockSpec((tm, d), lambda i, n: (i, 0), pipeline_mode=pl.Buffered(1)),
            pl.BlockSpec((1, d), row),
            pl.BlockSpec((d, tn), lambda i, n: (0, n)),
            pl.BlockSpec((d, nh), row),
            pl.BlockSpec((1, nh), row),
            pl.BlockSpec((1, HEAD_DIM), row),
            pl.BlockSpec((1, HEAD_DIM), row),
        ],
        out_specs=[
            pl.BlockSpec((tm, tn), sec_map(0)),
            pl.BlockSpec(heads_blk, head_map(1)),
            pl.BlockSpec((tm, tn), sec_map(1)),
            pl.BlockSpec(heads_blk, head_map(2)),
            pl.BlockSpec((tm, tn), sec_map(2)),
            pl.BlockSpec((tm, tn), sec_map(3)),
            pl.BlockSpec((tm, nh), lambda i, n: (i, 0)),
        ],
        out_shape=[bf_out, heads_out, bf_out, heads_out, bf_out, f32_out,
                   jax.ShapeDtypeStruct((t, nh), F32)],
        scratch_shapes=[pltpu.VMEM((tm, d), BF16)],
        compiler_params=_cparams(("parallel", "arbitrary")),
        name="inproj",
    )(x, g.reshape(1, d), w_main, w_f, b_f.reshape(1, nh),
      q_norm.reshape(1, HEAD_DIM), k_norm.reshape(1, HEAD_DIM))


def _split3(x):
    hi = x.astype(BF16)
    r1 = x - hi.astype(F32)
    mid = r1.astype(BF16)
    lo = (r1 - mid.astype(F32)).astype(BF16)
    return hi, mid, lo


def _cumsum_kernel(x_ref, o_ref, carry_ref):
    @pl.when(pl.program_id(0) == 0)
    def _():
        carry_ref[...] = jnp.zeros_like(carry_ref)

    x = x_ref[...]
    bs = x.shape[1]
    row = lax.broadcasted_iota(jnp.int32, (bs, bs), 0)
    col = lax.broadcasted_iota(jnp.int32, (bs, bs), 1)
    tri = jnp.where(row <= col, 1.0, 0.0).astype(BF16)
    hi, mid, lo = _split3(x)
    c = _dot(hi, tri) + _dot(mid, tri) + _dot(lo, tri) + carry_ref[...]
    o_ref[...] = c
    carry_ref[...] = c[:, bs - 1:bs]


def _cumsum_rows(x, *, bs):
    r, s = x.shape
    bs = min(bs, s)
    return pl.pallas_call(
        _cumsum_kernel,
        grid=(s // bs,),
        in_specs=[pl.BlockSpec((r, bs), lambda j: (0, j))],
        out_specs=pl.BlockSpec((r, bs), lambda j: (0, j)),
        out_shape=jax.ShapeDtypeStruct((r, s), F32),
        scratch_shapes=[pltpu.VMEM((r, 1), F32)],
        compiler_params=_cparams(("arbitrary",)),
        name="cumsum",
    )(x)


def _lane_tile(x, width):
    return x if width == LANES else jnp.tile(x, (1, width // LANES))


def _attn_prompt_kernel(qi_tab, ki_tab, q_ref, k_ref, v_ref, ccol_ref, crow_ref, o_ref,
                        m_ref, l_ref, acc_ref, cq_ref, *, heads, scale):
    t = pl.program_id(2)
    qi, ki = qi_tab[t], ki_tab[t]
    tq, tk = q_ref.shape[1], k_ref.shape[1]

    @pl.when(ki == 0)
    def _():
        m_ref[...] = jnp.full_like(m_ref, NEG_INF)
        l_ref[...] = jnp.zeros_like(l_ref)
        acc_ref[...] = jnp.zeros_like(acc_ref)
        for hh in range(heads):
            cq_ref[hh] = jnp.broadcast_to(ccol_ref[0, hh] * LOG2E, (tq, LANES))

    def step(masked):
        for hh in range(heads):
            sl = slice(hh * HEAD_DIM, (hh + 1) * HEAD_DIM)
            s = _dot_nt(q_ref[0, :, sl], k_ref[0, :, sl]) * (scale * LOG2E)
            s = s + _lane_tile(cq_ref[hh], tk) - crow_ref[0, hh] * LOG2E
            if masked:
                row = lax.broadcasted_iota(jnp.int32, (tq, tk), 0)
                col = lax.broadcasted_iota(jnp.int32, (tq, tk), 1)
                s = jnp.where(col <= row, s, NEG_INF)
            m_prev = m_ref[hh]
            m_new = jnp.maximum(m_prev, jnp.max(s, axis=-1, keepdims=True))
            alpha = jnp.exp2(m_prev - m_new)
            p = jnp.exp2(s - _lane_tile(m_new, tk))
            l_ref[hh] = alpha * l_ref[hh] + jnp.sum(p, axis=-1, keepdims=True)
            acc_ref[hh] = alpha * acc_ref[hh] + _dot(p.astype(BF16), v_ref[0, :, sl])
            m_ref[hh] = m_new

    @pl.when(ki < qi)
    def _():
        step(False)

    @pl.when(ki == qi)
    def _():
        step(True)
        for hh in range(heads):
            o_ref[0, :, hh * HEAD_DIM:(hh + 1) * HEAD_DIM] = acc_ref[hh] / l_ref[hh]


def _attn_prompt(q, kb, vb, c_t, *, tq, heads):
    b, s, da = q.shape
    nh = da // HEAD_DIM
    tq = min(tq, s)
    heads = min(heads, nh)
    nq = s // tq
    pairs = [(i, j) for i in range(nq) for j in range(i + 1)]
    qi_tab = jnp.asarray([p[0] for p in pairs], jnp.int32)
    ki_tab = jnp.asarray([p[1] for p in pairs], jnp.int32)
    w = heads * HEAD_DIM
    grid_spec = pltpu.PrefetchScalarGridSpec(
        num_scalar_prefetch=2,
        grid=(b, nh // heads, len(pairs)),
        in_specs=[
            pl.BlockSpec((1, tq, w), lambda bi, hg, t, qt, kt: (bi, qt[t], hg)),
            pl.BlockSpec((1, tq, w), lambda bi, hg, t, qt, kt: (bi, kt[t], hg)),
            pl.BlockSpec((1, tq, w), lambda bi, hg, t, qt, kt: (bi, kt[t], hg)),
            pl.BlockSpec((1, heads, tq, 1), lambda bi, hg, t, qt, kt: (bi, hg, qt[t], 0)),
            pl.BlockSpec((1, heads, 1, tq), lambda bi, hg, t, qt, kt: (bi, hg, 0, kt[t])),
        ],
        out_specs=pl.BlockSpec((1, tq, w), lambda bi, hg, t, qt, kt: (bi, qt[t], hg)),
        scratch_shapes=[
            pltpu.VMEM((heads, tq, LANES), F32),
            pltpu.VMEM((heads, tq, LANES), F32),
            pltpu.VMEM((heads, tq, HEAD_DIM), F32),
            pltpu.VMEM((heads, tq, LANES), F32),
        ],
    )
    return pl.pallas_call(
        functools.partial(_attn_prompt_kernel, heads=heads, scale=HEAD_DIM ** -0.5),
        grid_spec=grid_spec,
        out_shape=jax.ShapeDtypeStruct((b, s, da), F32),
        compiler_params=_cparams(("parallel", "parallel", "arbitrary")),
        name="attn_prompt",
    )(qi_tab, ki_tab, q, kb, vb, c_t.reshape(b, nh, s, 1), c_t.reshape(b, nh, 1, s))


def _attn_sample_kernel(q_ref, ck_ref, cv_ref, kn_ref, vn_ref, crow_ref, ccol_ref, o_ref, *, scale):
    t = q_ref.shape[1]
    past = ck_ref.shape[1]
    nh = q_ref.shape[2] // HEAD_DIM
    row = lax.broadcasted_iota(jnp.int32, (t, t), 0)
    col = lax.broadcasted_iota(jnp.int32, (t, t), 1)
    for hh in range(nh):
        sl = slice(hh * HEAD_DIM, (hh + 1) * HEAD_DIM)
        qh = q_ref[0, :, sl]
        cq = ccol_ref[0, :, hh:hh + 1]
        ck = crow_ref[0, hh:hh + 1, :]
        s1 = _dot_nt(qh, ck_ref[0, :, sl].astype(BF16)) * scale + cq - ck[:, :past]
        s2 = _dot_nt(qh, kn_ref[0, :, sl]) * scale + cq - ck[:, past:]
        s2 = jnp.where(col <= row, s2, NEG_INF)
        m = jnp.maximum(jnp.max(s1, axis=-1, keepdims=True), jnp.max(s2, axis=-1, keepdims=True))
        p1 = jnp.exp(s1 - m)
        p2 = jnp.exp(s2 - m)
        l = jnp.sum(p1, axis=-1, keepdims=True) + jnp.sum(p2, axis=-1, keepdims=True)
        o = _dot(p1.astype(BF16), cv_ref[0, :, sl].astype(BF16)) + _dot(p2.astype(BF16), vn_ref[0, :, sl])
        o_ref[0, :, sl] = o / l


def _attn_sample(q, cache_k, cache_v, kb, vb, c_t, c_col):
    b, t, da = q.shape
    past = cache_k.shape[1]
    nh = da // HEAD_DIM
    blk = lambda *shape: pl.BlockSpec((1,) + shape, lambda i: (i, 0, 0))
    return pl.pallas_call(
        functools.partial(_attn_sample_kernel, scale=HEAD_DIM ** -0.5),
        grid=(b,),
        in_specs=[blk(t, da), blk(past, da), blk(past, da), blk(t, da), blk(t, da),
                  blk(nh, past + t), blk(t, nh)],
        out_specs=blk(t, da),
        out_shape=jax.ShapeDtypeStruct((b, t, da), F32),
        compiler_params=_cparams(("parallel",)),
        name="attn_sample",
    )(q, cache_k, cache_v, kb, vb, c_t, c_col)


def _cexp(mag_log, ang):
    mag = jnp.exp(mag_log)
    return mag * jnp.cos(ang), mag * jnp.sin(ang)


def _cmul(ar, ai, br, bi):
    return ar * br - ai * bi, ar * bi + ai * br


def _gelu_tanh(x):
    return 0.5 * x * (1.0 + jnp.tanh(math.sqrt(2.0 / math.pi) * (x + 0.044715 * (x * x * x))))


def _split2(x):
    hi = x.astype(BF16)
    lo = (x - hi.astype(F32)).astype(BF16)
    return hi, lo


def _lane_block(rows):
    return lax.broadcasted_iota(jnp.int32, (rows, LANES), 1) // SSM_GROUP


def _block_transpose(xs):
    blk = _lane_block(xs[0].shape[0])
    s = CHUNK_T // 2
    while s:
        upper = (blk & s) != 0
        nxt = list(xs)
        for a in range(CHUNK_T):
            if a & s:
                continue
            lo, hi = xs[a], xs[a + s]
            nxt[a] = jnp.where(upper, pltpu.roll(hi, s * SSM_GROUP, axis=1), lo)
            nxt[a + s] = jnp.where(upper, hi, pltpu.roll(lo, (CHUNK_T - s) * SSM_GROUP, axis=1))
        xs = nxt
        s //= 2
    return xs


def _to_chunk_layout(src_ref, dst_ref):
    rows = dst_ref.shape[1]
    toks = [src_ref[pl.ds(t, rows, stride=CHUNK_T), :] for t in range(CHUNK_T)]
    for i, x in enumerate(_block_transpose(toks)):
        dst_ref[i] = x


def _from_chunk_layout(src_ref, dst_ref):
    rows = src_ref.shape[1]
    for t, x in enumerate(_block_transpose([src_ref[i] for i in range(CHUNK_T)])):
        dst_ref[pl.ds(t, rows, stride=CHUNK_T), :] = x


def _ssm_kernel(up_ref, us_ref, x0r_ref, x0i_ref, lrow_re_ref, lrow_im_ref, lcol_re_ref, lcol_im_ref,
                ldt_ref, l8_re_ref, l8_im_ref, ldt8_ref, bt_re_ref, bt_im_ref, ct_re_ref, ct_im_ref, d_ref,
                yp_ref, ys_ref, lastp_re_ref, lastp_im_ref, lasts_re_ref, lasts_im_ref,
                cp_ref, cs_ref, wr_ref, wi_ref, xr_ref, xi_ref, wsr_ref, wsi_ref, xsr_ref, xsi_ref,
                vre_ref, vim_ref, *, streams_p):
    j = pl.program_id(1)
    gb = GROUPS_PER_STEP
    rp = cp_ref.shape[1]
    p = SSM_STATE

    @pl.when(j == 0)
    def _():
        _to_chunk_layout(up_ref, cp_ref)
        _to_chunk_layout(us_ref, cs_ref)

    dt = jnp.exp(ldt_ref[0])

    a_row = lrow_re_ref[0] * dt
    w_row = lrow_im_ref[0] * dt
    s_idx = (lax.broadcasted_iota(jnp.int32, (LANES, p), 0) // SSM_GROUP).astype(F32)
    en_re, en_im = _cexp(-(s_idx + 1.0) * a_row, -(s_idx + 1.0) * w_row)
    l8_re, l8_im = _cexp(CHUNK_T * a_row, CHUNK_T * w_row)
    lb_re, lb_im = _cexp(a_row, w_row)
    lr, li = lrow_re_ref[0], lrow_im_ref[0]
    inv = 1.0 / (lr * lr + li * li)
    cf_re, cf_im = _cmul(lb_re - 1.0, lb_im, lr * inv, -li * inv)
    bb_re, bb_im = _cmul(cf_re, cf_im, bt_re_ref[0], bt_im_ref[0])
    f_re, f_im = _cmul(en_re, en_im, bb_re, bb_im)
    e7_re, e7_im = _cmul(en_re, en_im, l8_re, l8_im)
    w_re, w_im = _cmul(e7_re, e7_im, bb_re, bb_im)
    w_re, w_im = w_re.astype(BF16), w_im.astype(BF16)

    a_col = lcol_re_ref[0] * dt
    w_col = lcol_im_ref[0] * dt
    t_idx = (lax.broadcasted_iota(jnp.int32, (p, LANES), 1) // SSM_GROUP).astype(F32)
    et_re, et_im = _cexp((t_idx + 1.0) * a_col, (t_idx + 1.0) * w_col)
    g_re, g_im = _cmul(et_re, et_im, ct_re_ref[0], ct_im_ref[0])
    v_re, v_im = g_re.astype(BF16), (-g_im).astype(BF16)
    vre_ref[j] = v_re
    vim_ref[j] = v_im

    fs = jnp.concatenate([f_re, -f_im], axis=1)
    hs = jnp.concatenate([g_re, g_im], axis=0)
    fh, fl = _split2(fs)
    hh, hl = _split2(hs)
    tmat = _dot(fh, hh) + _dot(fh, hl) + _dot(fl, hh)
    srow = lax.broadcasted_iota(jnp.int32, (LANES, LANES), 0) // SSM_GROUP
    tcol = lax.broadcasted_iota(jnp.int32, (LANES, LANES), 1) // SSM_GROUP
    tmat = jnp.where(srow <= tcol, tmat, 0.0).astype(BF16)

    d_row = d_ref[0]

    u = cp_ref[j]
    ub = u.astype(BF16)
    cp_ref[j] = _dot(ub, tmat) + d_row * u
    base = pl.multiple_of(j * rp, 8)
    wr_ref[pl.ds(base, rp), :] = _dot(ub, w_re)
    wi_ref[pl.ds(base, rp), :] = _dot(ub, w_im)

    us = cs_ref[j]
    usb = us.astype(BF16)
    wsr_ref[...] = _dot(usb, w_re)
    wsi_ref[...] = _dot(usb, w_im)
    nb = x0r_ref.shape[1]
    ncs = us.shape[0] // nb
    sr, si = x0r_ref[0], x0i_ref[0]
    for c in range(ncs):
        rows = pl.ds(c, nb, stride=ncs)
        xsr_ref[rows, :] = sr
        xsi_ref[rows, :] = si
        nr, ni = _cmul(l8_re, l8_im, sr, si)
        sr = nr + wsr_ref[rows, :]
        si = ni + wsi_ref[rows, :]
    lasts_re_ref[0] = sr
    lasts_im_ref[0] = si
    xs_re = xsr_ref[...].astype(BF16)
    xs_im = xsi_ref[...].astype(BF16)
    cs_ref[j] = _gelu_tanh(_dot(usb, tmat) + _dot(xs_re, v_re) + _dot(xs_im, v_im) + d_row * us)

    @pl.when(j == gb - 1)
    def _():
        dt8 = jnp.exp(ldt8_ref[0])
        a8_re, a8_im = _cexp(CHUNK_T * l8_re_ref[0] * dt8, CHUNK_T * l8_im_ref[0] * dt8)
        nc = rp // streams_p

        def body(c, carry):
            new = []
            for b in range(streams_p):
                sr, si = carry[2 * b], carry[2 * b + 1]
                r = b * nc + c
                xr_ref[pl.ds(r, gb, stride=rp), :] = sr
                xi_ref[pl.ds(r, gb, stride=rp), :] = si
                nr, ni = _cmul(a8_re, a8_im, sr, si)
                new.append(nr + wr_ref[pl.ds(r, gb, stride=rp), :])
                new.append(ni + wi_ref[pl.ds(r, gb, stride=rp), :])
            return tuple(new)

        zero = jnp.zeros((gb, p), F32)
        fin = lax.fori_loop(0, nc, body, (zero,) * (2 * streams_p), unroll=4)
        for b in range(streams_p):
            lastp_re_ref[b] = fin[2 * b]
            lastp_im_ref[b] = fin[2 * b + 1]

        for g in range(gb):
            xr = xr_ref[g * rp:(g + 1) * rp, :].astype(BF16)
            xi = xi_ref[g * rp:(g + 1) * rp, :].astype(BF16)
            cp_ref[g] = _gelu_tanh(cp_ref[g] + _dot(xr, vre_ref[g]) + _dot(xi, vim_ref[g]))

        _from_chunk_layout(cp_ref, yp_ref)
        _from_chunk_layout(cs_ref, ys_ref)


def _ssm(up, us, x0_re, x0_im, lam_re, lam_im, log_dt, bt_re, bt_im, ct_re, ct_im, d_t, *, streams_p):
    tp, dssm = up.shape
    ts = us.shape[0]
    g = dssm // SSM_GROUP
    rp, rs = tp // CHUNK_T, ts // CHUNK_T
    nb = x0_re.shape[1]
    p = SSM_STATE
    gb = GROUPS_PER_STEP
    ngb = g // gb
    per_group = lambda *shape: pl.BlockSpec((1,) + shape, lambda i, j: (i * gb + j, 0, 0))
    per_block = lambda *shape: pl.BlockSpec((1,) + shape, lambda i, j: (i, 0, 0))
    tokens = lambda rows, **kw: pl.BlockSpec((rows, LANES), lambda i, j: (0, i), **kw)
    return pl.pallas_call(
        functools.partial(_ssm_kernel, streams_p=streams_p),
        grid=(ngb, gb),
        in_specs=[
            tokens(tp, pipeline_mode=pl.Buffered(1)), tokens(ts, pipeline_mode=pl.Buffered(1)),
            per_group(nb, p), per_group(nb, p),
            per_group(1, p), per_group(1, p), per_group(p, 1), per_group(p, 1), per_group(1, 1),
            per_block(gb, p), per_block(gb, p), per_block(gb, 1),
            per_group(LANES, p), per_group(LANES, p), per_group(p, LANES), per_group(p, LANES),
            per_group(1, LANES),
        ],
        out_specs=[
            tokens(tp), tokens(ts),
            pl.BlockSpec((streams_p, gb, p), lambda i, j: (0, i, 0)),
            pl.BlockSpec((streams_p, gb, p), lambda i, j: (0, i, 0)),
            per_group(nb, p), per_group(nb, p),
        ],
        out_shape=[
            jax.ShapeDtypeStruct((tp, dssm), F32),
            jax.ShapeDtypeStruct((ts, dssm), F32),
            jax.ShapeDtypeStruct((streams_p, g, p), F32),
            jax.ShapeDtypeStruct((streams_p, g, p), F32),
            jax.ShapeDtypeStruct((g, nb, p), F32),
            jax.ShapeDtypeStruct((g, nb, p), F32),
        ],
        scratch_shapes=[
            pltpu.VMEM((gb, rp, LANES), F32), pltpu.VMEM((gb, rs, LANES), F32),
            pltpu.VMEM((gb * rp, p), F32), pltpu.VMEM((gb * rp, p), F32),
            pltpu.VMEM((gb * rp, p), F32), pltpu.VMEM((gb * rp, p), F32),
            pltpu.VMEM((rs, p), F32), pltpu.VMEM((rs, p), F32),
            pltpu.VMEM((rs, p), F32), pltpu.VMEM((rs, p), F32),
            pltpu.VMEM((gb, p, LANES), BF16), pltpu.VMEM((gb, p, LANES), BF16),
        ],
        compiler_params=_cparams(("parallel", "arbitrary")),
        name="ssm",
    )(up, us, x0_re, x0_im,
      lam_re.reshape(g, 1, p), lam_im.reshape(g, 1, p), lam_re.reshape(g, p, 1), lam_im.reshape(g, p, 1),
      log_dt.reshape(g, 1, 1),
      lam_re.reshape(ngb, gb, p), lam_im.reshape(ngb, gb, p), log_dt.reshape(ngb, gb, 1),
      bt_re, bt_im, ct_re, ct_im, d_t)


def _glu_kernel(y_ref, w_ref, o_ref):
    y = y_ref[...]
    o_ref[...] = y * jax.nn.sigmoid(_dot(y.astype(BF16), w_ref[...]))


def _glu(y, w, *, tm):
    t, d = y.shape
    tm = min(tm, t)
    return pl.pallas_call(
        _glu_kernel,
        grid=(t // tm,),
        in_specs=[pl.BlockSpec((tm, d), lambda i: (i, 0)), pl.BlockSpec((d, d), lambda i: (0, 0))],
        out_specs=pl.BlockSpec((tm, d), lambda i: (i, 0)),
        out_shape=jax.ShapeDtypeStruct((t, d), F32),
        compiler_params=_cparams(("parallel",)),
        name="glu",
    )(y, w)


def _merge_kernel(o_ref, y_ref, x_ref, ga_ref, gs_ref, w_ref, out_ref, a_ref):
    da = o_ref.shape[1]

    @pl.when(pl.program_id(1) == 0)
    def _():
        a_ref[:, :da] = _rms(o_ref[...], ga_ref[...]).astype(BF16)
        a_ref[:, da:] = _rms(y_ref[...], gs_ref[...]).astype(BF16)

    out_ref[...] = x_ref[...] + _dot(a_ref[...], w_ref[...])


def _merge(o, y, x, g_attn, g_ssm, w_out, *, tm, tn):
    t, da = o.shape
    ds = y.shape[1]
    d = x.shape[1]
    tm, tn = min(tm, t), min(tn, d)
    return pl.pallas_call(
        _merge_kernel,
        grid=(t // tm, d // tn),
        in_specs=[
            pl.BlockSpec((tm, da), lambda i, n: (i, 0), pipeline_mode=pl.Buffered(1)),
            pl.BlockSpec((tm, ds), lambda i, n: (i, 0), pipeline_mode=pl.Buffered(1)),
            pl.BlockSpec((tm, tn), lambda i, n: (i, n)),
            pl.BlockSpec((1, da), lambda i, n: (0, 0)),
            pl.BlockSpec((1, ds), lambda i, n: (0, 0)),
            pl.BlockSpec((da + ds, tn), lambda i, n: (0, n)),
        ],
        out_specs=pl.BlockSpec((tm, tn), lambda i, n: (i, n)),
        out_shape=jax.ShapeDtypeStruct((t, d), F32),
        scratch_shapes=[pltpu.VMEM((tm, da + ds), BF16)],
        compiler_params=_cparams(("parallel", "arbitrary")),
        name="merge",
    )(o, y, x, g_attn.reshape(1, da), g_ssm.reshape(1, ds), w_out)


TILES = dict(
    ffn=dict(tm=1024, tf=256, tm_down=512, tn=512),
    inproj=dict(tm=512, tn=1024),
    attn_prompt=dict(tq=512, heads=4),
    cumsum=dict(bs=512),
    glu=dict(tm=512),
    merge=dict(tm=512, tn=1024),
)


def kernel(x_prompt, x_sample, cache_k, cache_v, cache_logf, state_ssm_re, state_ssm_im, norm_ffn1, w1_a, w3_a, w2_a, norm_mix, w_in, b_f, q_norm, k_norm, lam_re, lam_im, log_dt, b_re, b_im, c_re, c_im, d_skip, w_glu, out_norm_attn, out_norm_ssm, w_out, norm_ffn2, w1_b, w3_b, w2_b):
    depth = norm_ffn1.shape[0]
    assert depth == 1
    l = 0
    bp, sp, d = x_prompt.shape
    bs, ts, _ = x_sample.shape
    past = cache_k.shape[2]
    nh = b_f.shape[1]
    da = nh * HEAD_DIM
    groups = lam_re.shape[1]
    dssm = groups * SSM_GROUP
    p = SSM_STATE

    bf = lambda w: w.astype(BF16)
    w_main = bf(jnp.concatenate([w_in[l][:, :3 * da], w_in[l][:, 3 * da + nh:]], axis=1))
    w_f = bf(w_in[l][:, 3 * da:3 * da + nh])

    xp = x_prompt.reshape(bp * sp, d)
    xs = x_sample.reshape(bs * ts, d)

    ffn_a = functools.partial(_ffn, g=norm_ffn1[l], w1=bf(w1_a[l]), w3=bf(w3_a[l]), w2=bf(w2_a[l]), **TILES["ffn"])
    xp = ffn_a(xp)
    xs = ffn_a(xs)

    proj = functools.partial(_inproj, g=norm_mix[l], w_main=w_main, w_f=w_f, b_f=b_f[l],
                             q_norm=q_norm[l], k_norm=k_norm[l], **TILES["inproj"])
    qp, kp, kbp, vp, vbp, up, lfp = proj(xp)
    qs, ks, kbs, vs, vbs, us, lfs = proj(xs)

    lfp_t = lfp.reshape(bp, sp, nh).transpose(0, 2, 1).reshape(bp * nh, sp)
    cp_t = _cumsum_rows(lfp_t, **TILES["cumsum"]).reshape(bp, nh, sp)
    o_p = _attn_prompt(qp.reshape(bp, sp, da), kbp.reshape(bp, sp, da), vbp.reshape(bp, sp, da), cp_t,
                       **TILES["attn_prompt"])

    lfs_all = jnp.concatenate([cache_logf[l].astype(F32), lfs.reshape(bs, ts, nh)], axis=1)
    lfs_t = lfs_all.transpose(0, 2, 1).reshape(bs * nh, past + ts)
    cs_t = _cumsum_rows(lfs_t, bs=past + ts).reshape(bs, nh, past + ts)
    cs_col = cs_t[:, :, past:].transpose(0, 2, 1)
    o_s = _attn_sample(qs.reshape(bs, ts, da), cache_k[l].reshape(bs, past, da), cache_v[l].reshape(bs, past, da),
                       kbs.reshape(bs, ts, da), vbs.reshape(bs, ts, da), cs_t, cs_col)

    bt = lambda b: jnp.tile(b.transpose(0, 2, 1), (1, CHUNK_T, 1))
    ct = lambda c: jnp.tile(c.transpose(0, 2, 1), (1, 1, CHUNK_T))
    d_t = jnp.tile(d_skip[l].reshape(groups, 1, SSM_GROUP), (1, 1, CHUNK_T))
    yp, ys, lp_re, lp_im, ls_re, ls_im = _ssm(
        up, us,
        state_ssm_re[l].astype(F32).transpose(1, 0, 2), state_ssm_im[l].astype(F32).transpose(1, 0, 2),
        lam_re[l], lam_im[l], log_dt[l], bt(b_re[l]), bt(b_im[l]), ct(c_re[l]), ct(c_im[l]), d_t,
        streams_p=bp)

    glu = functools.partial(_glu, w=bf(w_glu[l]), **TILES["glu"])
    merge = functools.partial(_merge, g_attn=out_norm_attn[l], g_ssm=out_norm_ssm[l], w_out=bf(w_out[l]),
                              **TILES["merge"])
    xp = merge(o_p.reshape(bp * sp, da), glu(yp), xp)
    xs = merge(o_s.reshape(bs * ts, da), glu(ys), xs)

    ffn_b = functools.partial(_ffn, g=norm_ffn2[l], w1=bf(w1_b[l]), w3=bf(w3_b[l]), w2=bf(w2_b[l]), **TILES["ffn"])
    xp = ffn_b(xp)
    xs = ffn_b(xs)

    return (xp.reshape(bp, sp, d), xs.reshape(bs, ts, d),
            kp.reshape(1, bp, sp, nh, HEAD_DIM), vp.reshape(1, bp, sp, nh, HEAD_DIM), lfp.reshape(1, bp, sp, nh),
            lp_re[None], lp_im[None],
            ks.reshape(1, bs, ts, nh, HEAD_DIM), vs.reshape(1, bs, ts, nh, HEAD_DIM), lfs.reshape(1, bs, ts, nh),
            ls_re.transpose(1, 0, 2)[None], ls_im.transpose(1, 0, 2)[None])
```

```python
import functools
import math

import jax
import jax.numpy as jnp
import numpy as np
from jax import lax
from jax.experimental import pallas as pl
from jax.experimental.pallas import tpu as pltpu

F32 = jnp.float32
BF16 = jnp.bfloat16

EPS = 1e-6
NEG_INF = -1e30
LOG2E = math.log2(math.e)
HEAD_DIM = 128
SSM_GROUP = 16
SSM_STATE = 64
LANES = 128
MXU_COLS = 256
CHUNK_T = LANES // SSM_GROUP
GROUPS_PER_STEP = 8
VMEM_LIMIT = 56 * 1024 * 1024


def _cparams(sem):
    return pltpu.CompilerParams(dimension_semantics=sem, vmem_limit_bytes=VMEM_LIMIT)


def _rms(x, g):
    ms = jnp.mean(x * x, axis=-1, keepdims=True)
    return x * lax.rsqrt(ms + EPS) * g


def _dot(a, b):
    return jnp.dot(a, b, preferred_element_type=F32)


def _dot_nt(a, b):
    return lax.dot_general(a, b, (((1,), (1,)), ((), ())), preferred_element_type=F32)


def _ffn_up_kernel(x_ref, g_ref, w1_ref, w3_ref, act_ref, h_ref):
    @pl.when(pl.program_id(1) == 0)
    def _():
        h_ref[...] = _rms(x_ref[...], g_ref[...]).astype(BF16)

    h = h_ref[...]
    a = _dot(h, w1_ref[...])
    b = _dot(h, w3_ref[...])
    act_ref[...] = (0.5 * a * jax.nn.sigmoid(a) * b).astype(BF16)


def _ffn_down_kernel(act_ref, w2_ref, x_ref, o_ref):
    o_ref[...] = x_ref[...] + _dot(act_ref[...], w2_ref[...])


def _ffn(x, g, w1, w3, w2, *, tm, tf, tm_down, tn):
    t, d = x.shape
    f = w1.shape[1]
    tm, tf, tn = min(tm, t), min(tf, f), min(tn, d)
    once = pl.Buffered(1)
    act = pl.pallas_call(
        _ffn_up_kernel,
        grid=(t // tm, f // tf),
        in_specs=[
            pl.BlockSpec((tm, d), lambda i, j: (i, 0), pipeline_mode=once),
            pl.BlockSpec((1, d), lambda i, j: (0, 0)),
            pl.BlockSpec((d, tf), lambda i, j: (0, j)),
            pl.BlockSpec((d, tf), lambda i, j: (0, j)),
        ],
        out_specs=pl.BlockSpec((tm, tf), lambda i, j: (i, j)),
        out_shape=jax.ShapeDtypeStruct((t, f), BF16),
        scratch_shapes=[pltpu.VMEM((tm, d), BF16)],
        compiler_params=_cparams(("parallel", "arbitrary")),
        name="ffn_up",
    )(x, g.reshape(1, d), w1, w3)
    tm = min(tm_down, t)
    return pl.pallas_call(
        _ffn_down_kernel,
        grid=(t // tm, d // tn),
        in_specs=[
            pl.BlockSpec((tm, f), lambda i, n: (i, 0)),
            pl.BlockSpec((f, tn), lambda i, n: (0, n)),
            pl.BlockSpec((tm, tn), lambda i, n: (i, n)),
        ],
        out_specs=pl.BlockSpec((tm, tn), lambda i, n: (i, n)),
        out_shape=jax.ShapeDtypeStruct((t, d), F32),
        compiler_params=_cparams(("parallel", "arbitrary")),
        name="ffn_down",
    )(act, w2, x)


def _inproj_kernel(x_ref, g_ref, w_ref, wu_ref, wf_ref, bf_ref, qn_ref, kn_ref,
                   q_ref, k_ref, kb_ref, v_ref, vb_ref, u_ref, lf_ref, h_ref, *, tiles):
    n = pl.program_id(1)

    @pl.when(n == 0)
    def _():
        h = _rms(x_ref[...], g_ref[...]).astype(BF16)
        h_ref[...] = h
        zf = _dot(h, wf_ref[...]) + bf_ref[...]
        lf_ref[...] = jnp.minimum(zf, 0.0) - jnp.log1p(jnp.exp(-jnp.abs(zf)))

    sec = n // tiles
    tn = w_ref.shape[1]
    cw = min(MXU_COLS, tn)

    def chunks(weights):
        for c in range(tn // cw):
            cols = slice(c * cw, (c + 1) * cw)
            yield cols, _dot(h_ref[...], weights[:, cols])

    def head_norm(z, gain):
        outs = [_rms(z[:, hh * HEAD_DIM:(hh + 1) * HEAD_DIM], gain) for hh in range(cw // HEAD_DIM)]
        return outs[0] if len(outs) == 1 else jnp.concatenate(outs, axis=-1)

    @pl.when(sec == 0)
    def _():
        for cols, z in chunks(w_ref):
            q_ref[:, cols] = head_norm(z, qn_ref[...]).astype(BF16)

    @pl.when(sec == 1)
    def _():
        for cols, z in chunks(w_ref):
            kk = head_norm(z, kn_ref[...])
            k_ref[:, cols] = kk
            kb_ref[:, cols] = kk.astype(BF16)

    @pl.when(sec == 2)
    def _():
        for cols, z in chunks(w_ref):
            v_ref[:, cols] = z
            vb_ref[:, cols] = z.astype(BF16)

    @pl.when(sec >= 3)
    def _():
        for cols, z in chunks(wu_ref):
            u_ref[:, cols] = z


def _inproj(x, g, w_qkv, w_u, w_f, b_f, q_norm, k_norm, *, tm, tn):
    t, d = x.shape
    da = w_qkv.shape[1] // 3
    du = w_u.shape[1]
    nh = w_f.shape[1]
    tm, tn = min(tm, t), min(tn, da, du)
    tiles, tiles_u = da // tn, du // tn

    def sec_map(s, count):
        return lambda i, n: (i, jnp.clip(n - s * tiles, 0, count - 1))

    row = lambda i, n: (0, 0)
    attn_out = lambda dt: jax.ShapeDtypeStruct((t, da), dt)
    return pl.pallas_call(
        functools.partial(_inproj_kernel, tiles=tiles),
        grid=(t // tm, 3 * tiles + tiles_u),
        in_specs=[
            pl.BlockSpec((tm, d), lambda i, n: (i, 0), pipeline_mode=pl.Buffered(1)),
            pl.BlockSpec((1, d), row),
            pl.BlockSpec((d, tn), lambda i, n: (0, jnp.minimum(n, 3 * tiles - 1))),
            pl.BlockSpec((d, tn), lambda i, n: (0, jnp.clip(n - 3 * tiles, 0, tiles_u - 1))),
            pl.BlockSpec((d, nh), row),
            pl.BlockSpec((1, nh), row),
            pl.BlockSpec((1, HEAD_DIM), row),
            pl.BlockSpec((1, HEAD_DIM), row),
        ],
        out_specs=[
            pl.BlockSpec((tm, tn), sec_map(0, tiles)),
            pl.BlockSpec((tm, tn), sec_map(1, tiles)),
            pl.BlockSpec((tm, tn), sec_map(1, tiles)),
            pl.BlockSpec((tm, tn), sec_map(2, tiles)),
            pl.BlockSpec((tm, tn), sec_map(2, tiles)),
            pl.BlockSpec((tm, tn), sec_map(3, tiles_u)),
            pl.BlockSpec((tm, nh), lambda i, n: (i, 0)),
        ],
        out_shape=[attn_out(BF16), attn_out(F32), attn_out(BF16), attn_out(F32), attn_out(BF16),
                   jax.ShapeDtypeStruct((t, du), F32), jax.ShapeDtypeStruct((t, nh), F32)],
        scratch_shapes=[pltpu.VMEM((tm, d), BF16)],
        compiler_params=_cparams(("parallel", "arbitrary")),
        name="inproj",
    )(x, g.reshape(1, d), w_qkv, w_u, w_f, b_f.reshape(1, nh),
      q_norm.reshape(1, HEAD_DIM), k_norm.reshape(1, HEAD_DIM))


def _split3(x):
    hi = x.astype(BF16)
    r1 = x - hi.astype(F32)
    mid = r1.astype(BF16)
    lo = (r1 - mid.astype(F32)).astype(BF16)
    return hi, mid, lo


def _cumsum_kernel(x_ref, o_ref, carry_ref):
    @pl.when(pl.program_id(0) == 0)
    def _():
        carry_ref[...] = jnp.zeros_like(carry_ref)

    x = x_ref[...]
    bs = x.shape[1]
    row = lax.broadcasted_iota(jnp.int32, (bs, bs), 0)
    col = lax.broadcasted_iota(jnp.int32, (bs, bs), 1)
    tri = jnp.where(row <= col, 1.0, 0.0).astype(BF16)
    hi, mid, lo = _split3(x)
    c = _dot(hi, tri) + _dot(mid, tri) + _dot(lo, tri) + carry_ref[...]
    o_ref[...] = c
    carry_ref[...] = c[:, bs - 1:bs]


def _cumsum_rows(x, *, bs):
    r, s = x.shape
    bs = min(bs, s)
    return pl.pallas_call(
        _cumsum_kernel,
        grid=(s // bs,),
        in_specs=[pl.BlockSpec((r, bs), lambda j: (0, j))],
        out_specs=pl.BlockSpec((r, bs), lambda j: (0, j)),
        out_shape=jax.ShapeDtypeStruct((r, s), F32),
        scratch_shapes=[pltpu.VMEM((r, 1), F32)],
        compiler_params=_cparams(("arbitrary",)),
        name="cumsum",
    )(x)


def _lane_tile(x, width):
    return x if width == LANES else jnp.tile(x, (1, width // LANES))


def _attn_prompt_kernel(qi_tab, ki_tab, q_ref, k_ref, v_ref, ccol_ref, crow_ref, o_ref,
                        m_ref, l_ref, acc_ref, cq_ref, *, heads, scale):
    t = pl.program_id(2)
    qi, ki = qi_tab[t], ki_tab[t]
    tq, tk = q_ref.shape[1], k_ref.shape[1]

    @pl.when(ki == 0)
    def _():
        m_ref[...] = jnp.full_like(m_ref, NEG_INF)
        l_ref[...] = jnp.zeros_like(l_ref)
        acc_ref[...] = jnp.zeros_like(acc_ref)
        for hh in range(heads):
            cq_ref[hh] = jnp.broadcast_to(ccol_ref[0, hh] * LOG2E, (tq, LANES))

    def step(masked):
        for hh in range(heads):
            sl = slice(hh * HEAD_DIM, (hh + 1) * HEAD_DIM)
            s = _dot_nt(q_ref[0, :, sl], k_ref[0, :, sl]) * (scale * LOG2E)
            s = s + _lane_tile(cq_ref[hh], tk) - crow_ref[0, hh] * LOG2E
            if masked:
                row = lax.broadcasted_iota(jnp.int32, (tq, tk), 0)
                col = lax.broadcasted_iota(jnp.int32, (tq, tk), 1)
                s = jnp.where(col <= row, s, NEG_INF)
            m_prev = m_ref[hh]
            m_new = jnp.maximum(m_prev, jnp.max(s, axis=-1, keepdims=True))
            alpha = jnp.exp2(m_prev - m_new)
            p = jnp.exp2(s - _lane_tile(m_new, tk))
            l_ref[hh] = alpha * l_ref[hh] + jnp.sum(p, axis=-1, keepdims=True)
            acc_ref[hh] = alpha * acc_ref[hh] + _dot(p.astype(BF16), v_ref[0, :, sl])
            m_ref[hh] = m_new

    @pl.when(ki < qi)
    def _():
        step(False)

    @pl.when(ki == qi)
    def _():
        step(True)
        for hh in range(heads):
            o_ref[0, :, hh * HEAD_DIM:(hh + 1) * HEAD_DIM] = acc_ref[hh] / l_ref[hh]


def _attn_prompt(q, kb, vb, c_t, *, tq, heads):
    b, s, da = q.shape
    nh = da // HEAD_DIM
    tq = min(tq, s)
    heads = min(heads, nh)
    nq = s // tq
    pairs = [(i, j) for i in range(nq) for j in range(i + 1)]
    qi_tab = jnp.asarray([p[0] for p in pairs], jnp.int32)
    ki_tab = jnp.asarray([p[1] for p in pairs], jnp.int32)
    w = heads * HEAD_DIM
    grid_spec = pltpu.PrefetchScalarGridSpec(
        num_scalar_prefetch=2,
        grid=(b, nh // heads, len(pairs)),
        in_specs=[
            pl.BlockSpec((1, tq, w), lambda bi, hg, t, qt, kt: (bi, qt[t], hg)),
            pl.BlockSpec((1, tq, w), lambda bi, hg, t, qt, kt: (bi, kt[t], hg)),
            pl.BlockSpec((1, tq, w), lambda bi, hg, t, qt, kt: (bi, kt[t], hg)),
            pl.BlockSpec((1, heads, tq, 1), lambda bi, hg, t, qt, kt: (bi, hg, qt[t], 0)),
            pl.BlockSpec((1, heads, 1, tq), lambda bi, hg, t, qt, kt: (bi, hg, 0, kt[t])),
        ],
        out_specs=pl.BlockSpec((1, tq, w), lambda bi, hg, t, qt, kt: (bi, qt[t], hg)),
        scratch_shapes=[
            pltpu.VMEM((heads, tq, LANES), F32),
            pltpu.VMEM((heads, tq, LANES), F32),
            pltpu.VMEM((heads, tq, HEAD_DIM), F32),
            pltpu.VMEM((heads, tq, LANES), F32),
        ],
    )
    return pl.pallas_call(
        functools.partial(_attn_prompt_kernel, heads=heads, scale=HEAD_DIM ** -0.5),
        grid_spec=grid_spec,
        out_shape=jax.ShapeDtypeStruct((b, s, da), F32),
        compiler_params=_cparams(("parallel", "parallel", "arbitrary")),
        name="attn_prompt",
    )(qi_tab, ki_tab, q, kb, vb, c_t.reshape(b, nh, s, 1), c_t.reshape(b, nh, 1, s))


def _attn_sample_kernel(q_ref, ck_ref, cv_ref, kn_ref, vn_ref, crow_ref, ccol_ref, o_ref, *, scale):
    t = q_ref.shape[1]
    nh = q_ref.shape[2] // HEAD_DIM
    past = ck_ref.shape[1] // nh
    row = lax.broadcasted_iota(jnp.int32, (t, t), 0)
    col = lax.broadcasted_iota(jnp.int32, (t, t), 1)
    for hh in range(nh):
        sl = slice(hh * HEAD_DIM, (hh + 1) * HEAD_DIM)
        cached = pl.ds(hh, past, stride=nh)
        qh = q_ref[0, :, sl]
        cq = ccol_ref[0, :, hh:hh + 1]
        ck = crow_ref[0, hh:hh + 1, :]
        s1 = _dot_nt(qh, ck_ref[0, cached, :].astype(BF16)) * scale + cq - ck[:, :past]
        s2 = _dot_nt(qh, kn_ref[0, :, sl]) * scale + cq - ck[:, past:]
        s2 = jnp.where(col <= row, s2, NEG_INF)
        m = jnp.maximum(jnp.max(s1, axis=-1, keepdims=True), jnp.max(s2, axis=-1, keepdims=True))
        p1 = jnp.exp(s1 - m)
        p2 = jnp.exp(s2 - m)
        l = jnp.sum(p1, axis=-1, keepdims=True) + jnp.sum(p2, axis=-1, keepdims=True)
        o = _dot(p1.astype(BF16), cv_ref[0, cached, :].astype(BF16)) + _dot(p2.astype(BF16), vn_ref[0, :, sl])
        o_ref[0, :, sl] = o / l


def _attn_sample(q, cache_k, cache_v, kb, vb, c_t, c_col):
    b, t, da = q.shape
    nh = da // HEAD_DIM
    rows = cache_k.shape[1]
    blk = lambda *shape: pl.BlockSpec((1,) + shape, lambda i: (i, 0, 0))
    return pl.pallas_call(
        functools.partial(_attn_sample_kernel, scale=HEAD_DIM ** -0.5),
        grid=(b,),
        in_specs=[blk(t, da), blk(rows, HEAD_DIM), blk(rows, HEAD_DIM), blk(t, da), blk(t, da),
                  blk(nh, rows // nh + t), blk(t, nh)],
        out_specs=blk(t, da),
        out_shape=jax.ShapeDtypeStruct((b, t, da), F32),
        compiler_params=_cparams(("parallel",)),
        name="attn_sample",
    )(q, cache_k, cache_v, kb, vb, c_t, c_col)


def _cexp(mag_log, ang):
    mag = jnp.exp(mag_log)
    return mag * jnp.cos(ang), mag * jnp.sin(ang)


def _cmul(ar, ai, br, bi):
    return ar * br - ai * bi, ar * bi + ai * br


def _gelu_tanh(x):
    return 0.5 * x * (1.0 + jnp.tanh(math.sqrt(2.0 / math.pi) * (x + 0.044715 * (x * x * x))))


def _split2(x):
    hi = x.astype(BF16)
    lo = (x - hi.astype(F32)).astype(BF16)
    return hi, lo


def _lane_block(rows):
    return lax.broadcasted_iota(jnp.int32, (rows, LANES), 1) // SSM_GROUP


def _block_transpose(xs):
    blk = _lane_block(xs[0].shape[0])
    s = CHUNK_T // 2
    while s:
        upper = (blk & s) != 0
        nxt = list(xs)
        for a in range(CHUNK_T):
            if a & s:
                continue
            lo, hi = xs[a], xs[a + s]
            nxt[a] = jnp.where(upper, pltpu.roll(hi, s * SSM_GROUP, axis=1), lo)
            nxt[a + s] = jnp.where(upper, hi, pltpu.roll(lo, (CHUNK_T - s) * SSM_GROUP, axis=1))
        xs = nxt
        s //= 2
    return xs


def _to_chunk_layout(src_ref, dst_ref):
    rows = dst_ref.shape[1]
    toks = [src_ref[pl.ds(t, rows, stride=CHUNK_T), :] for t in range(CHUNK_T)]
    for i, x in enumerate(_block_transpose(toks)):
        dst_ref[i] = x


def _from_chunk_layout(src_ref, dst_ref):
    rows = src_ref.shape[1]
    for t, x in enumerate(_block_transpose([src_ref[i] for i in range(CHUNK_T)])):
        dst_ref[pl.ds(t, rows, stride=CHUNK_T), :] = x


def _ssm_kernel(up_ref, us_ref, x0r_ref, x0i_ref, lrow_re_ref, lrow_im_ref, lcol_re_ref, lcol_im_ref,
                ldt_ref, l8_re_ref, l8_im_ref, ldt8_ref, bt_re_ref, bt_im_ref, ct_re_ref, ct_im_ref, d_ref,
                yp_ref, ys_ref, lastp_re_ref, lastp_im_ref, lasts_re_ref, lasts_im_ref,
                cp_ref, cs_ref, wr_ref, wi_ref, xr_ref, xi_ref, wsr_ref, wsi_ref, xsr_ref, xsi_ref,
                vre_ref, vim_ref, *, streams_p):
    j = pl.program_id(1)
    gb = GROUPS_PER_STEP
    rp = cp_ref.shape[1]
    p = SSM_STATE

    @pl.when(j == 0)
    def _():
        _to_chunk_layout(up_ref, cp_ref)
        _to_chunk_layout(us_ref, cs_ref)

    dt = jnp.exp(ldt_ref[0])

    a_row = lrow_re_ref[0] * dt
    w_row = lrow_im_ref[0] * dt
    s_idx = (lax.broadcasted_iota(jnp.int32, (LANES, p), 0) // SSM_GROUP).astype(F32)
    en_re, en_im = _cexp(-(s_idx + 1.0) * a_row, -(s_idx + 1.0) * w_row)
    l8_re, l8_im = _cexp(CHUNK_T * a_row, CHUNK_T * w_row)
    lb_re, lb_im = _cexp(a_row, w_row)
    lr, li = lrow_re_ref[0], lrow_im_ref[0]
    inv = 1.0 / (lr * lr + li * li)
    cf_re, cf_im = _cmul(lb_re - 1.0, lb_im, lr * inv, -li * inv)
    bb_re, bb_im = _cmul(cf_re, cf_im, bt_re_ref[0], bt_im_ref[0])
    f_re, f_im = _cmul(en_re, en_im, bb_re, bb_im)
    e7_re, e7_im = _cmul(en_re, en_im, l8_re, l8_im)
    w_re, w_im = _cmul(e7_re, e7_im, bb_re, bb_im)
    w_re, w_im = w_re.astype(BF16), w_im.astype(BF16)

    a_col = lcol_re_ref[0] * dt
    w_col = lcol_im_ref[0] * dt
    t_idx = (lax.broadcasted_iota(jnp.int32, (p, LANES), 1) // SSM_GROUP).astype(F32)
    et_re, et_im = _cexp((t_idx + 1.0) * a_col, (t_idx + 1.0) * w_col)
    g_re, g_im = _cmul(et_re, et_im, ct_re_ref[0], ct_im_ref[0])
    v_re, v_im = g_re.astype(BF16), (-g_im).astype(BF16)
    vre_ref[j] = v_re
    vim_ref[j] = v_im

    fs = jnp.concatenate([f_re, -f_im], axis=1)
    hs = jnp.concatenate([g_re, g_im], axis=0)
    fh, fl = _split2(fs)
    hh, hl = _split2(hs)
    tmat = _dot(fh, hh) + _dot(fh, hl) + _dot(fl, hh)
    srow = lax.broadcasted_iota(jnp.int32, (LANES, LANES), 0) // SSM_GROUP
    tcol = lax.broadcasted_iota(jnp.int32, (LANES, LANES), 1) // SSM_GROUP
    tmat = jnp.where(srow <= tcol, tmat, 0.0).astype(BF16)

    d_row = d_ref[0]

    u = cp_ref[j]
    ub = u.astype(BF16)
    cp_ref[j] = _dot(ub, tmat) + d_row * u
    base = pl.multiple_of(j * rp, 8)
    wr_ref[pl.ds(base, rp), :] = _dot(ub, w_re)
    wi_ref[pl.ds(base, rp), :] = _dot(ub, w_im)

    us = cs_ref[j]
    usb = us.astype(BF16)
    wsr_ref[...] = _dot(usb, w_re)
    wsi_ref[...] = _dot(usb, w_im)
    nb = x0r_ref.shape[1]
    ncs = us.shape[0] // nb
    sr, si = x0r_ref[0], x0i_ref[0]
    for c in range(ncs):
        rows = pl.ds(c, nb, stride=ncs)
        xsr_ref[rows, :] = sr
        xsi_ref[rows, :] = si
        nr, ni = _cmul(l8_re, l8_im, sr, si)
        sr = nr + wsr_ref[rows, :]
        si = ni + wsi_ref[rows, :]
    lasts_re_ref[0] = sr
    lasts_im_ref[0] = si
    xs_re = xsr_ref[...].astype(BF16)
    xs_im = xsi_ref[...].astype(BF16)
    cs_ref[j] = _gelu_tanh(_dot(usb, tmat) + _dot(xs_re, v_re) + _dot(xs_im, v_im) + d_row * us)

    @pl.when(j == gb - 1)
    def _():
        dt8 = jnp.exp(ldt8_ref[0])
        a8_re, a8_im = _cexp(CHUNK_T * l8_re_ref[0] * dt8, CHUNK_T * l8_im_ref[0] * dt8)
        nc = rp // streams_p

        def body(c, carry):
            new = []
            for b in range(streams_p):
                sr, si = carry[2 * b], carry[2 * b + 1]
                r = b * nc + c
                xr_ref[pl.ds(r, gb, stride=rp), :] = sr
                xi_ref[pl.ds(r, gb, stride=rp), :] = si
                nr, ni = _cmul(a8_re, a8_im, sr, si)
                new.append(nr + wr_ref[pl.ds(r, gb, stride=rp), :])
                new.append(ni + wi_ref[pl.ds(r, gb, stride=rp), :])
            return tuple(new)

        zero = jnp.zeros((gb, p), F32)
        fin = lax.fori_loop(0, nc, body, (zero,) * (2 * streams_p), unroll=4)
        for b in range(streams_p):
            lastp_re_ref[b] = fin[2 * b]
            lastp_im_ref[b] = fin[2 * b + 1]

        for g in range(gb):
            xr = xr_ref[g * rp:(g + 1) * rp, :].astype(BF16)
            xi = xi_ref[g * rp:(g + 1) * rp, :].astype(BF16)
            cp_ref[g] = _gelu_tanh(cp_ref[g] + _dot(xr, vre_ref[g]) + _dot(xi, vim_ref[g]))

        _from_chunk_layout(cp_ref, yp_ref)
        _from_chunk_layout(cs_ref, ys_ref)


def _ssm(up, us, x0_re, x0_im, lam_re, lam_im, log_dt, bt_re, bt_im, ct_re, ct_im, d_t, *, streams_p):
    tp, dssm = up.shape
    ts = us.shape[0]
    g = dssm // SSM_GROUP
    rp, rs = tp // CHUNK_T, ts // CHUNK_T
    nb = x0_re.shape[1]
    p = SSM_STATE
    gb = GROUPS_PER_STEP
    ngb = g // gb
    per_group = lambda *shape: pl.BlockSpec((1,) + shape, lambda i, j: (i * gb + j, 0, 0))
    per_block = lambda *shape: pl.BlockSpec((1,) + shape, lambda i, j: (i, 0, 0))
    tokens = lambda rows, **kw: pl.BlockSpec((rows, LANES), lambda i, j: (0, i), **kw)
    return pl.pallas_call(
        functools.partial(_ssm_kernel, streams_p=streams_p),
        grid=(ngb, gb),
        in_specs=[
            tokens(tp, pipeline_mode=pl.Buffered(1)), tokens(ts, pipeline_mode=pl.Buffered(1)),
            per_group(nb, p), per_group(nb, p),
            per_group(1, p), per_group(1, p), per_group(p, 1), per_group(p, 1), per_group(1, 1),
            per_block(gb, p), per_block(gb, p), per_block(gb, 1),
            per_group(LANES, p), per_group(LANES, p), per_group(p, LANES), per_group(p, LANES),
            per_group(1, LANES),
        ],
        out_specs=[
            tokens(tp), tokens(ts),
            pl.BlockSpec((streams_p, gb, p), lambda i, j: (0, i, 0)),
            pl.BlockSpec((streams_p, gb, p), lambda i, j: (0, i, 0)),
            per_group(nb, p), per_group(nb, p),
        ],
        out_shape=[
            jax.ShapeDtypeStruct((tp, dssm), F32),
            jax.ShapeDtypeStruct((ts, dssm), F32),
            jax.ShapeDtypeStruct((streams_p, g, p), F32),
            jax.ShapeDtypeStruct((streams_p, g, p), F32),
            jax.ShapeDtypeStruct((g, nb, p), F32),
            jax.ShapeDtypeStruct((g, nb, p), F32),
        ],
        scratch_shapes=[
            pltpu.VMEM((gb, rp, LANES), F32), pltpu.VMEM((gb, rs, LANES), F32),
            pltpu.VMEM((gb * rp, p), F32), pltpu.VMEM((gb * rp, p), F32),
            pltpu.VMEM((gb * rp, p), F32), pltpu.VMEM((gb * rp, p), F32),
            pltpu.VMEM((rs, p), F32), pltpu.VMEM((rs, p), F32),
            pltpu.VMEM((rs, p), F32), pltpu.VMEM((rs, p), F32),
            pltpu.VMEM((gb, p, LANES), BF16), pltpu.VMEM((gb, p, LANES), BF16),
        ],
        compiler_params=_cparams(("parallel", "arbitrary")),
        name="ssm",
    )(up, us, x0_re, x0_im,
      lam_re.reshape(g, 1, p), lam_im.reshape(g, 1, p), lam_re.reshape(g, p, 1), lam_im.reshape(g, p, 1),
      log_dt.reshape(g, 1, 1),
      lam_re.reshape(ngb, gb, p), lam_im.reshape(ngb, gb, p), log_dt.reshape(ngb, gb, 1),
      bt_re, bt_im, ct_re, ct_im, d_t)


def _glu_kernel(y_ref, w_ref, o_ref):
    y = y_ref[...]
    o_ref[...] = y * jax.nn.sigmoid(_dot(y.astype(BF16), w_ref[...]))


def _glu(y, w, *, tm):
    t, d = y.shape
    tm = min(tm, t)
    return pl.pallas_call(
        _glu_kernel,
        grid=(t // tm,),
        in_specs=[pl.BlockSpec((tm, d), lambda i: (i, 0)), pl.BlockSpec((d, d), lambda i: (0, 0))],
        out_specs=pl.BlockSpec((tm, d), lambda i: (i, 0)),
        out_shape=jax.ShapeDtypeStruct((t, d), F32),
        compiler_params=_cparams(("parallel",)),
        name="glu",
    )(y, w)


def _merge_kernel(o_ref, y_ref, x_ref, ga_ref, gs_ref, w_ref, out_ref, a_ref):
    da = o_ref.shape[1]

    @pl.when(pl.program_id(1) == 0)
    def _():
        a_ref[:, :da] = _rms(o_ref[...], ga_ref[...]).astype(BF16)
        a_ref[:, da:] = _rms(y_ref[...], gs_ref[...]).astype(BF16)

    out_ref[...] = x_ref[...] + _dot(a_ref[...], w_ref[...])


def _merge(o, y, x, g_attn, g_ssm, w_out, *, tm, tn):
    t, da = o.shape
    ds = y.shape[1]
    d = x.shape[1]
    tm, tn = min(tm, t), min(tn, d)
    return pl.pallas_call(
        _merge_kernel,
        grid=(t // tm, d // tn),
        in_specs=[
            pl.BlockSpec((tm, da), lambda i, n: (i, 0), pipeline_mode=pl.Buffered(1)),
            pl.BlockSpec((tm, ds), lambda i, n: (i, 0), pipeline_mode=pl.Buffered(1)),
            pl.BlockSpec((tm, tn), lambda i, n: (i, n)),
            pl.BlockSpec((1, da), lambda i, n: (0, 0)),
            pl.BlockSpec((1, ds), lambda i, n: (0, 0)),
            pl.BlockSpec((da + ds, tn), lambda i, n: (0, n)),
        ],
        out_specs=pl.BlockSpec((tm, tn), lambda i, n: (i, n)),
        out_shape=jax.ShapeDtypeStruct((t, d), F32),
        scratch_shapes=[pltpu.VMEM((tm, da + ds), BF16)],
        compiler_params=_cparams(("parallel", "arbitrary")),
        name="merge",
    )(o, y, x, g_attn.reshape(1, da), g_ssm.reshape(1, ds), w_out)


TILES = dict(
    ffn=dict(tm=1024, tf=256, tm_down=512, tn=512),
    inproj=dict(tm=512, tn=512),
    attn_prompt=dict(tq=512, heads=4),
    cumsum=dict(bs=512),
    glu=dict(tm=512),
    merge=dict(tm=512, tn=1024),
)


def kernel(x_prompt, x_sample, cache_k, cache_v, cache_logf, state_ssm_re, state_ssm_im, norm_ffn1, w1_a, w3_a, w2_a, norm_mix, w_in, b_f, q_norm, k_norm, lam_re, lam_im, log_dt, b_re, b_im, c_re, c_im, d_skip, w_glu, out_norm_attn, out_norm_ssm, w_out, norm_ffn2, w1_b, w3_b, w2_b):
    depth = norm_ffn1.shape[0]
    assert depth == 1
    l = 0
    bp, sp, d = x_prompt.shape
    bs, ts, _ = x_sample.shape
    past = cache_k.shape[2]
    nh = b_f.shape[1]
    da = nh * HEAD_DIM
    groups = lam_re.shape[1]
    dssm = groups * SSM_GROUP
    p = SSM_STATE

    bf = lambda w: w.astype(BF16)
    w_qkv = bf(w_in[l][:, :3 * da])
    w_f = bf(w_in[l][:, 3 * da:3 * da + nh])
    w_u = bf(w_in[l][:, 3 * da + nh:])

    xp = x_prompt.reshape(bp * sp, d)
    xs = x_sample.reshape(bs * ts, d)

    ffn_a = functools.partial(_ffn, g=norm_ffn1[l], w1=bf(w1_a[l]), w3=bf(w3_a[l]), w2=bf(w2_a[l]), **TILES["ffn"])
    xp = ffn_a(xp)
    xs = ffn_a(xs)

    proj = functools.partial(_inproj, g=norm_mix[l], w_qkv=w_qkv, w_u=w_u, w_f=w_f, b_f=b_f[l],
                             q_norm=q_norm[l], k_norm=k_norm[l], **TILES["inproj"])
    qp, kp, kbp, vp, vbp, up, lfp = proj(xp)
    qs, ks, kbs, vs, vbs, us, lfs = proj(xs)

    lfp_t = lfp.reshape(bp, sp, nh).transpose(0, 2, 1).reshape(bp * nh, sp)
    cp_t = _cumsum_rows(lfp_t, **TILES["cumsum"]).reshape(bp, nh, sp)
    o_p = _attn_prompt(qp.reshape(bp, sp, da), kbp.reshape(bp, sp, da), vbp.reshape(bp, sp, da), cp_t,
                       **TILES["attn_prompt"])

    lfs_all = jnp.concatenate([cache_logf[l].astype(F32), lfs.reshape(bs, ts, nh)], axis=1)
    lfs_t = lfs_all.transpose(0, 2, 1).reshape(bs * nh, past + ts)
    cs_t = _cumsum_rows(lfs_t, bs=past + ts).reshape(bs, nh, past + ts)
    cs_col = cs_t[:, :, past:].transpose(0, 2, 1)
    o_s = _attn_sample(qs.reshape(bs, ts, da),
                       cache_k[l].reshape(bs, past * nh, HEAD_DIM), cache_v[l].reshape(bs, past * nh, HEAD_DIM),
                       kbs.reshape(bs, ts, da), vbs.reshape(bs, ts, da), cs_t, cs_col)

    bt = lambda b: jnp.tile(b.transpose(0, 2, 1), (1, CHUNK_T, 1))
    ct = lambda c: jnp.tile(c.transpose(0, 2, 1), (1, 1, CHUNK_T))
    d_t = jnp.tile(d_skip[l].reshape(groups, 1, SSM_GROUP), (1, 1, CHUNK_T))
    yp, ys, lp_re, lp_im, ls_re, ls_im = _ssm(
        up, us,
        state_ssm_re[l].astype(F32).transpose(1, 0, 2), state_ssm_im[l].astype(F32).transpose(1, 0, 2),
        lam_re[l], lam_im[l], log_dt[l], bt(b_re[l]), bt(b_im[l]), ct(c_re[l]), ct(c_im[l]), d_t,
        streams_p=bp)

    glu = functools.partial(_glu, w=bf(w_glu[l]), **TILES["glu"])
    merge = functools.partial(_merge, g_attn=out_norm_attn[l], g_ssm=out_norm_ssm[l], w_out=bf(w_out[l]),
                              **TILES["merge"])
    xp = merge(o_p.reshape(bp * sp, da), glu(yp), xp)
    xs = merge(o_s.reshape(bs * ts, da), glu(ys), xs)

    ffn_b = functools.partial(_ffn, g=norm_ffn2[l], w1=bf(w1_b[l]), w3=bf(w3_b[l]), w2=bf(w2_b[l]), **TILES["ffn"])
    xp = ffn_b(xp)
    xs = ffn_b(xs)

    return (xp.reshape(bp, sp, d), xs.reshape(bs, ts, d),
            kp.reshape(1, bp, sp, nh, HEAD_DIM), vp.reshape(1, bp, sp, nh, HEAD_DIM), lfp.reshape(1, bp, sp, nh),
            lp_re[None], lp_im[None],
            ks.reshape(1, bs, ts, nh, HEAD_DIM), vs.reshape(1, bs, ts, nh, HEAD_DIM), lfs.reshape(1, bs, ts, nh),
            ls_re.transpose(1, 0, 2)[None], ls_im.transpose(1, 0, 2)[None])
```

```python
import functools
import math

import jax
import jax.numpy as jnp
import numpy as np
from jax import lax
from jax.experimental import pallas as pl
from jax.experimental.pallas import tpu as pltpu

F32 = jnp.float32
BF16 = jnp.bfloat16

EPS = 1e-6
NEG_INF = -1e30
LOG2E = math.log2(math.e)
HEAD_DIM = 128
SSM_GROUP = 16
SSM_STATE = 64
LANES = 128
MXU_COLS = 256
CHUNK_T = LANES // SSM_GROUP
GROUPS_PER_STEP = 8
VMEM_LIMIT = 56 * 1024 * 1024


def _cparams(sem):
    return pltpu.CompilerParams(dimension_semantics=sem, vmem_limit_bytes=VMEM_LIMIT)


def _rms(x, g):
    ms = jnp.mean(x * x, axis=-1, keepdims=True)
    return x * lax.rsqrt(ms + EPS) * g


def _dot(a, b):
    return jnp.dot(a, b, preferred_element_type=F32)


def _dot_nt(a, b):
    return lax.dot_general(a, b, (((1,), (1,)), ((), ())), preferred_element_type=F32)


def _ffn_up_kernel(x_ref, g_ref, w1_ref, w3_ref, act_ref, h_ref):
    @pl.when(pl.program_id(1) == 0)
    def _():
        h_ref[...] = _rms(x_ref[...], g_ref[...]).astype(BF16)

    h = h_ref[...]
    a = _dot(h, w1_ref[...])
    b = _dot(h, w3_ref[...])
    act_ref[...] = (0.5 * a * jax.nn.sigmoid(a) * b).astype(BF16)


def _ffn_down_kernel(act_ref, w2_ref, x_ref, o_ref):
    o_ref[...] = x_ref[...] + _dot(act_ref[...], w2_ref[...])


def _ffn(x, g, w1, w3, w2, *, tm, tf, tm_down, tn):
    t, d = x.shape
    f = w1.shape[1]
    tm, tf, tn = min(tm, t), min(tf, f), min(tn, d)
    once = pl.Buffered(1)
    act = pl.pallas_call(
        _ffn_up_kernel,
        grid=(t // tm, f // tf),
        in_specs=[
            pl.BlockSpec((tm, d), lambda i, j: (i, 0), pipeline_mode=once),
            pl.BlockSpec((1, d), lambda i, j: (0, 0)),
            pl.BlockSpec((d, tf), lambda i, j: (0, j)),
            pl.BlockSpec((d, tf), lambda i, j: (0, j)),
        ],
        out_specs=pl.BlockSpec((tm, tf), lambda i, j: (i, j)),
        out_shape=jax.ShapeDtypeStruct((t, f), BF16),
        scratch_shapes=[pltpu.VMEM((tm, d), BF16)],
        compiler_params=_cparams(("parallel", "arbitrary")),
        name="ffn_up",
    )(x, g.reshape(1, d), w1, w3)
    tm = min(tm_down, t)
    return pl.pallas_call(
        _ffn_down_kernel,
        grid=(t // tm, d // tn),
        in_specs=[
            pl.BlockSpec((tm, f), lambda i, n: (i, 0)),
            pl.BlockSpec((f, tn), lambda i, n: (0, n)),
            pl.BlockSpec((tm, tn), lambda i, n: (i, n)),
        ],
        out_specs=pl.BlockSpec((tm, tn), lambda i, n: (i, n)),
        out_shape=jax.ShapeDtypeStruct((t, d), F32),
        compiler_params=_cparams(("parallel", "arbitrary")),
        name="ffn_down",
    )(act, w2, x)


def _split_w_in_kernel(a_ref, b_ref, qkv_ref, f_ref, u_ref, *, nq):
    j = pl.program_id(0)
    nh = f_ref.shape[1]

    @pl.when(j < nq)
    def _():
        qkv_ref[...] = a_ref[...].astype(BF16)

    @pl.when(j == nq)
    def _():
        f_ref[...] = a_ref[:, :nh].astype(BF16)

    @pl.when(j >= nq)
    def _():
        u_ref[...] = jnp.concatenate([a_ref[:, nh:], b_ref[:, :nh]], axis=1).astype(BF16)


def _split_w_in(w, da, nh, *, cb):
    d, n_in = w.shape
    du = n_in - 3 * da - nh
    cb = math.gcd(math.gcd(3 * da, du), cb)
    nq, nu = 3 * da // cb, du // cb
    return pl.pallas_call(
        functools.partial(_split_w_in_kernel, nq=nq),
        grid=(nq + nu,),
        in_specs=[
            pl.BlockSpec((d, cb), lambda j: (0, j)),
            pl.BlockSpec((d, cb), lambda j: (0, jnp.maximum(j + 1, nq + 1))),
        ],
        out_specs=[
            pl.BlockSpec((d, cb), lambda j: (0, jnp.minimum(j, nq - 1))),
            pl.BlockSpec((d, nh), lambda j: (0, 0)),
            pl.BlockSpec((d, cb), lambda j: (0, jnp.maximum(j - nq, 0))),
        ],
        out_shape=[jax.ShapeDtypeStruct((d, 3 * da), BF16), jax.ShapeDtypeStruct((d, nh), BF16),
                   jax.ShapeDtypeStruct((d, du), BF16)],
        compiler_params=_cparams(("arbitrary",)),
        name="split_w_in",
    )(w, w)

def _inproj_kernel(x_ref, g_ref, w_ref, wu_ref, wf_ref, bf_ref, qn_ref, kn_ref,
                   q_ref, k_ref, kb_ref, v_ref, vb_ref, u_ref, lf_ref, h_ref, *, tiles):
    n = pl.program_id(1)

    @pl.when(n == 0)
    def _():
        h = _rms(x_ref[...], g_ref[...]).astype(BF16)
        h_ref[...] = h
        zf = _dot(h, wf_ref[...]) + bf_ref[...]
        lf_ref[...] = jnp.minimum(zf, 0.0) - jnp.log1p(jnp.exp(-jnp.abs(zf)))

    sec = n // tiles
    tn = w_ref.shape[1]
    cw = min(MXU_COLS, tn)

    def chunks(weights):
        for c in range(tn // cw):
            cols = slice(c * cw, (c + 1) * cw)
            yield cols, _dot(h_ref[...], weights[:, cols])

    def head_norm(z, gain):
        outs = [_rms(z[:, hh * HEAD_DIM:(hh + 1) * HEAD_DIM], gain) for hh in range(cw // HEAD_DIM)]
        return outs[0] if len(outs) == 1 else jnp.concatenate(outs, axis=-1)

    @pl.when(sec == 0)
    def _():
        for cols, z in chunks(w_ref):
            q_ref[:, cols] = head_norm(z, qn_ref[...]).astype(BF16)

    @pl.when(sec == 1)
    def _():
        for cols, z in chunks(w_ref):
            kk = head_norm(z, kn_ref[...])
            k_ref[:, cols] = kk
            kb_ref[:, cols] = kk.astype(BF16)

    @pl.when(sec == 2)
    def _():
        for cols, z in chunks(w_ref):
            v_ref[:, cols] = z
            vb_ref[:, cols] = z.astype(BF16)

    @pl.when(sec >= 3)
    def _():
        for cols, z in chunks(wu_ref):
            u_ref[:, cols] = z


def _inproj(x, g, w_qkv, w_u, w_f, b_f, q_norm, k_norm, *, tm, tn):
    t, d = x.shape
    da = w_qkv.shape[1] // 3
    du = w_u.shape[1]
    nh = w_f.shape[1]
    tm, tn = min(tm, t), min(tn, da, du)
    tiles, tiles_u = da // tn, du // tn

    def sec_map(s, count):
        return lambda i, n: (i, jnp.clip(n - s * tiles, 0, count - 1))

    row = lambda i, n: (0, 0)
    attn_out = lambda dt: jax.ShapeDtypeStruct((t, da), dt)
    return pl.pallas_call(
        functools.partial(_inproj_kernel, tiles=tiles),
        grid=(t // tm, 3 * tiles + tiles_u),
        in_specs=[
            pl.BlockSpec((tm, d), lambda i, n: (i, 0), pipeline_mode=pl.Buffered(1)),
            pl.BlockSpec((1, d), row),
            pl.BlockSpec((d, tn), lambda i, n: (0, jnp.minimum(n, 3 * tiles - 1))),
            pl.BlockSpec((d, tn), lambda i, n: (0, jnp.clip(n - 3 * tiles, 0, tiles_u - 1))),
            pl.BlockSpec((d, nh), row),
            pl.BlockSpec((1, nh), row),
            pl.BlockSpec((1, HEAD_DIM), row),
            pl.BlockSpec((1, HEAD_DIM), row),
        ],
        out_specs=[
            pl.BlockSpec((tm, tn), sec_map(0, tiles)),
            pl.BlockSpec((tm, tn), sec_map(1, tiles)),
            pl.BlockSpec((tm, tn), sec_map(1, tiles)),
            pl.BlockSpec((tm, tn), sec_map(2, tiles)),
            pl.BlockSpec((tm, tn), sec_map(2, tiles)),
            pl.BlockSpec((tm, tn), sec_map(3, tiles_u)),
            pl.BlockSpec((tm, nh), lambda i, n: (i, 0)),
        ],
        out_shape=[attn_out(BF16), attn_out(F32), attn_out(BF16), attn_out(F32), attn_out(BF16),
                   jax.ShapeDtypeStruct((t, du), F32), jax.ShapeDtypeStruct((t, nh), F32)],
        scratch_shapes=[pltpu.VMEM((tm, d), BF16)],
        compiler_params=_cparams(("parallel", "arbitrary")),
        name="inproj",
    )(x, g.reshape(1, d), w_qkv, w_u, w_f, b_f.reshape(1, nh),
      q_norm.reshape(1, HEAD_DIM), k_norm.reshape(1, HEAD_DIM))


def _split3(x):
    hi = x.astype(BF16)
    r1 = x - hi.astype(F32)
    mid = r1.astype(BF16)
    lo = (r1 - mid.astype(F32)).astype(BF16)
    return hi, mid, lo


def _cumsum_kernel(x_ref, o_ref, carry_ref):
    @pl.when(pl.program_id(0) == 0)
    def _():
        carry_ref[...] = jnp.zeros_like(carry_ref)

    x = x_ref[...]
    bs = x.shape[1]
    row = lax.broadcasted_iota(jnp.int32, (bs, bs), 0)
    col = lax.broadcasted_iota(jnp.int32, (bs, bs), 1)
    tri = jnp.where(row <= col, 1.0, 0.0).astype(BF16)
    hi, mid, lo = _split3(x)
    c = _dot(hi, tri) + _dot(mid, tri) + _dot(lo, tri) + carry_ref[...]
    o_ref[...] = c
    carry_ref[...] = c[:, bs - 1:bs]


def _cumsum_rows(x, *, bs):
    r, s = x.shape
    bs = min(bs, s)
    return pl.pallas_call(
        _cumsum_kernel,
        grid=(s // bs,),
        in_specs=[pl.BlockSpec((r, bs), lambda j: (0, j))],
        out_specs=pl.BlockSpec((r, bs), lambda j: (0, j)),
        out_shape=jax.ShapeDtypeStruct((r, s), F32),
        scratch_shapes=[pltpu.VMEM((r, 1), F32)],
        compiler_params=_cparams(("arbitrary",)),
        name="cumsum",
    )(x)


def _lane_tile(x, width):
    return x if width == LANES else jnp.tile(x, (1, width // LANES))


def _attn_prompt_kernel(qi_tab, ki_tab, q_ref, k_ref, v_ref, ccol_ref, crow_ref, o_ref,
                        m_ref, l_ref, acc_ref, cq_ref, *, heads, scale):
    t = pl.program_id(2)
    qi, ki = qi_tab[t], ki_tab[t]
    tq, tk = q_ref.shape[1], k_ref.shape[1]

    @pl.when(ki == 0)
    def _():
        m_ref[...] = jnp.full_like(m_ref, NEG_INF)
        l_ref[...] = jnp.zeros_like(l_ref)
        acc_ref[...] = jnp.zeros_like(acc_ref)
        for hh in range(heads):
            cq_ref[hh] = jnp.broadcast_to(ccol_ref[0, hh] * LOG2E, (tq, LANES))

    def step(masked):
        for hh in range(heads):
            sl = slice(hh * HEAD_DIM, (hh + 1) * HEAD_DIM)
            s = _dot_nt(q_ref[0, :, sl], k_ref[0, :, sl]) * (scale * LOG2E)
            s = s + _lane_tile(cq_ref[hh], tk) - crow_ref[0, hh] * LOG2E
            if masked:
                row = lax.broadcasted_iota(jnp.int32, (tq, tk), 0)
                col = lax.broadcasted_iota(jnp.int32, (tq, tk), 1)
                s = jnp.where(col <= row, s, NEG_INF)
            m_prev = m_ref[hh]
            m_new = jnp.maximum(m_prev, jnp.max(s, axis=-1, keepdims=True))
            alpha = jnp.exp2(m_prev - m_new)
            p = jnp.exp2(s - _lane_tile(m_new, tk))
            l_ref[hh] = alpha * l_ref[hh] + jnp.sum(p, axis=-1, keepdims=True)
            acc_ref[hh] = alpha * acc_ref[hh] + _dot(p.astype(BF16), v_ref[0, :, sl])
            m_ref[hh] = m_new

    @pl.when(ki < qi)
    def _():
        step(False)

    @pl.when(ki == qi)
    def _():
        step(True)
        for hh in range(heads):
            o_ref[0, :, hh * HEAD_DIM:(hh + 1) * HEAD_DIM] = acc_ref[hh] / l_ref[hh]


def _attn_prompt(q, kb, vb, c_t, *, tq, heads):
    b, s, da = q.shape
    nh = da // HEAD_DIM
    tq = min(tq, s)
    heads = min(heads, nh)
    nq = s // tq
    pairs = [(i, j) for i in range(nq) for j in range(i + 1)]
    qi_tab = jnp.asarray([p[0] for p in pairs], jnp.int32)
    ki_tab = jnp.asarray([p[1] for p in pairs], jnp.int32)
    w = heads * HEAD_DIM
    grid_spec = pltpu.PrefetchScalarGridSpec(
        num_scalar_prefetch=2,
        grid=(b, nh // heads, len(pairs)),
        in_specs=[
            pl.BlockSpec((1, tq, w), lambda bi, hg, t, qt, kt: (bi, qt[t], hg)),
            pl.BlockSpec((1, tq, w), lambda bi, hg, t, qt, kt: (bi, kt[t], hg)),
            pl.BlockSpec((1, tq, w), lambda bi, hg, t, qt, kt: (bi, kt[t], hg)),
            pl.BlockSpec((1, heads, tq, 1), lambda bi, hg, t, qt, kt: (bi, hg, qt[t], 0)),
            pl.BlockSpec((1, heads, 1, tq), lambda bi, hg, t, qt, kt: (bi, hg, 0, kt[t])),
        ],
        out_specs=pl.BlockSpec((1, tq, w), lambda bi, hg, t, qt, kt: (bi, qt[t], hg)),
        scratch_shapes=[
            pltpu.VMEM((heads, tq, LANES), F32),
            pltpu.VMEM((heads, tq, LANES), F32),
            pltpu.VMEM((heads, tq, HEAD_DIM), F32),
            pltpu.VMEM((heads, tq, LANES), F32),
        ],
    )
    return pl.pallas_call(
        functools.partial(_attn_prompt_kernel, heads=heads, scale=HEAD_DIM ** -0.5),
        grid_spec=grid_spec,
        out_shape=jax.ShapeDtypeStruct((b, s, da), F32),
        compiler_params=_cparams(("parallel", "parallel", "arbitrary")),
        name="attn_prompt",
    )(qi_tab, ki_tab, q, kb, vb, c_t.reshape(b, nh, s, 1), c_t.reshape(b, nh, 1, s))


def _attn_sample_kernel(q_ref, ck_ref, cv_ref, kn_ref, vn_ref, crow_ref, ccol_ref, o_ref, kq_ref, vq_ref, *, scale):
    t = q_ref.shape[1]
    nh = q_ref.shape[2] // HEAD_DIM
    past = ck_ref.shape[1] // nh
    row = lax.broadcasted_iota(jnp.int32, (t, t), 0)
    col = lax.broadcasted_iota(jnp.int32, (t, t), 1)

    s1 = kq_ref.shape[0]
    s2 = nh // s1
    for r in range(s1):
        kq_ref[r] = ck_ref[0, pl.ds(r, past * s2, stride=s1), :]
        vq_ref[r] = cv_ref[0, pl.ds(r, past * s2, stride=s1), :]

    for hh in range(nh):
        sl = slice(hh * HEAD_DIM, (hh + 1) * HEAD_DIM)
        r, a = hh % s1, hh // s1
        cached = pl.ds(a, past, stride=s2) if s2 > 1 else slice(None)
        qh = q_ref[0, :, sl]
        cq = ccol_ref[0, :, hh:hh + 1]
        ck = crow_ref[0, hh:hh + 1, :]
        sc = _dot_nt(qh, kq_ref[r, cached, :].astype(BF16)) * scale + cq - ck[:, :past]
        sn = _dot_nt(qh, kn_ref[0, :, sl]) * scale + cq - ck[:, past:]
        sn = jnp.where(col <= row, sn, NEG_INF)
        m = jnp.maximum(jnp.max(sc, axis=-1, keepdims=True), jnp.max(sn, axis=-1, keepdims=True))
        pc = jnp.exp(sc - m)
        pn = jnp.exp(sn - m)
        l = jnp.sum(pc, axis=-1, keepdims=True) + jnp.sum(pn, axis=-1, keepdims=True)
        o = _dot(pc.astype(BF16), vq_ref[r, cached, :].astype(BF16)) + _dot(pn.astype(BF16), vn_ref[0, :, sl])
        o_ref[0, :, sl] = o / l


def _attn_sample(q, cache_k, cache_v, kb, vb, c_t, c_col):
    b, t, da = q.shape
    nh = da // HEAD_DIM
    rows = cache_k.shape[1]
    first_stride = math.gcd(nh, 4)
    blk = lambda *shape: pl.BlockSpec((1,) + shape, lambda i: (i, 0, 0))
    return pl.pallas_call(
        functools.partial(_attn_sample_kernel, scale=HEAD_DIM ** -0.5),
        grid=(b,),
        in_specs=[blk(t, da), blk(rows, HEAD_DIM), blk(rows, HEAD_DIM), blk(t, da), blk(t, da),
                  blk(nh, rows // nh + t), blk(t, nh)],
        out_specs=blk(t, da),
        out_shape=jax.ShapeDtypeStruct((b, t, da), F32),
        scratch_shapes=[pltpu.VMEM((first_stride, rows // first_stride, HEAD_DIM), F32)] * 2,
        compiler_params=_cparams(("parallel",)),
        name="attn_sample",
    )(q, cache_k, cache_v, kb, vb, c_t, c_col)


def _cexp(mag_log, ang):
    mag = jnp.exp(mag_log)
    return mag * jnp.cos(ang), mag * jnp.sin(ang)


def _cmul(ar, ai, br, bi):
    return ar * br - ai * bi, ar * bi + ai * br


def _cpow_table(zr, zi, k):
    acc_r = jnp.broadcast_to(zr, k.shape)
    acc_i = jnp.broadcast_to(zi, k.shape)
    bit = 1
    while bit < CHUNK_T:
        take = (k & bit) != 0
        acc_r, acc_i = _cmul(acc_r, acc_i, jnp.where(take, zr, 1.0), jnp.where(take, zi, 0.0))
        zr, zi = _cmul(zr, zi, zr, zi)
        bit *= 2
    return acc_r, acc_i


def _gelu_tanh(x):
    return 0.5 * x * (1.0 + jnp.tanh(math.sqrt(2.0 / math.pi) * (x + 0.044715 * (x * x * x))))


def _split2(x):
    hi = x.astype(BF16)
    lo = (x - hi.astype(F32)).astype(BF16)
    return hi, lo


def _lane_block(rows):
    return lax.broadcasted_iota(jnp.int32, (rows, LANES), 1) // SSM_GROUP


def _block_transpose(xs):
    blk = _lane_block(xs[0].shape[0])
    s = CHUNK_T // 2
    while s:
        upper = (blk & s) != 0
        nxt = list(xs)
        for a in range(CHUNK_T):
            if a & s:
                continue
            lo, hi = xs[a], xs[a + s]
            nxt[a] = jnp.where(upper, pltpu.roll(hi, s * SSM_GROUP, axis=1), lo)
            nxt[a + s] = jnp.where(upper, hi, pltpu.roll(lo, (CHUNK_T - s) * SSM_GROUP, axis=1))
        xs = nxt
        s //= 2
    return xs


def _to_chunk_layout(src_ref, dst_ref):
    rows = dst_ref.shape[1]
    toks = [src_ref[pl.ds(t, rows, stride=CHUNK_T), :] for t in range(CHUNK_T)]
    for i, x in enumerate(_block_transpose(toks)):
        dst_ref[i] = x


def _from_chunk_layout(src_ref, dst_ref):
    rows = src_ref.shape[1]
    for t, x in enumerate(_block_transpose([src_ref[i] for i in range(CHUNK_T)])):
        dst_ref[pl.ds(t, rows, stride=CHUNK_T), :] = x


def _ssm_kernel(up_ref, us_ref, x0r_ref, x0i_ref, lrow_re_ref, lrow_im_ref, lcol_re_ref, lcol_im_ref,
                ldt_ref, l8_re_ref, l8_im_ref, ldt8_ref, bt_re_ref, bt_im_ref, ct_re_ref, ct_im_ref, d_ref,
                yp_ref, ys_ref, lastp_re_ref, lastp_im_ref, lasts_re_ref, lasts_im_ref,
                cp_ref, cs_ref, wr_ref, wi_ref, xr_ref, xi_ref, wsr_ref, wsi_ref, xsr_ref, xsi_ref,
                vre_ref, vim_ref, *, streams_p):
    j = pl.program_id(1)
    gb = GROUPS_PER_STEP
    rp = cp_ref.shape[1]
    p = SSM_STATE

    @pl.when(j == 0)
    def _():
        _to_chunk_layout(up_ref, cp_ref)
        _to_chunk_layout(us_ref, cs_ref)

    dt = jnp.exp(ldt_ref[0])

    a_row = lrow_re_ref[0] * dt
    w_row = lrow_im_ref[0] * dt
    s_idx = lax.broadcasted_iota(jnp.int32, (LANES, p), 0) // SSM_GROUP
    en_re, en_im = _cpow_table(*_cexp(-a_row, -w_row), s_idx)
    l8_re, l8_im = _cexp(CHUNK_T * a_row, CHUNK_T * w_row)
    lb_re, lb_im = _cexp(a_row, w_row)
    lr, li = lrow_re_ref[0], lrow_im_ref[0]
    inv = 1.0 / (lr * lr + li * li)
    cf_re, cf_im = _cmul(lb_re - 1.0, lb_im, lr * inv, -li * inv)
    bb_re, bb_im = _cmul(cf_re, cf_im, bt_re_ref[0], bt_im_ref[0])
    f_re, f_im = _cmul(en_re, en_im, bb_re, bb_im)
    e7_re, e7_im = _cmul(en_re, en_im, l8_re, l8_im)
    w_re, w_im = _cmul(e7_re, e7_im, bb_re, bb_im)
    w_re, w_im = w_re.astype(BF16), w_im.astype(BF16)

    a_col = lcol_re_ref[0] * dt
    w_col = lcol_im_ref[0] * dt
    t_idx = lax.broadcasted_iota(jnp.int32, (p, LANES), 1) // SSM_GROUP
    et_re, et_im = _cpow_table(*_cexp(a_col, w_col), t_idx)
    g_re, g_im = _cmul(et_re, et_im, ct_re_ref[0], ct_im_ref[0])
    v_re, v_im = g_re.astype(BF16), (-g_im).astype(BF16)
    vre_ref[j] = v_re
    vim_ref[j] = v_im

    fs = jnp.concatenate([f_re, -f_im], axis=1)
    hs = jnp.concatenate([g_re, g_im], axis=0)
    fh, fl = _split2(fs)
    hh, hl = _split2(hs)
    tmat = _dot(fh, hh) + _dot(fh, hl) + _dot(fl, hh)
    srow = lax.broadcasted_iota(jnp.int32, (LANES, LANES), 0) // SSM_GROUP
    tcol = lax.broadcasted_iota(jnp.int32, (LANES, LANES), 1) // SSM_GROUP
    tmat = jnp.where(srow <= tcol, tmat, 0.0).astype(BF16)

    d_row = d_ref[0]

    u = cp_ref[j]
    ub = u.astype(BF16)
    cp_ref[j] = _dot(ub, tmat) + d_row * u
    base = pl.multiple_of(j * rp, 8)
    wr_ref[pl.ds(base, rp), :] = _dot(ub, w_re)
    wi_ref[pl.ds(base, rp), :] = _dot(ub, w_im)

    us = cs_ref[j]
    usb = us.astype(BF16)
    wsr_ref[...] = _dot(usb, w_re)
    wsi_ref[...] = _dot(usb, w_im)
    nb = x0r_ref.shape[1]
    ncs = us.shape[0] // nb
    sr, si = x0r_ref[0], x0i_ref[0]
    for c in range(ncs):
        rows = pl.ds(c, nb, stride=ncs)
        xsr_ref[rows, :] = sr
        xsi_ref[rows, :] = si
        nr, ni = _cmul(l8_re, l8_im, sr, si)
        sr = nr + wsr_ref[rows, :]
        si = ni + wsi_ref[rows, :]
    lasts_re_ref[0] = sr
    lasts_im_ref[0] = si
    xs_re = xsr_ref[...].astype(BF16)
    xs_im = xsi_ref[...].astype(BF16)
    cs_ref[j] = _gelu_tanh(_dot(usb, tmat) + _dot(xs_re, v_re) + _dot(xs_im, v_im) + d_row * us)

    @pl.when(j == gb - 1)
    def _():
        dt8 = jnp.exp(ldt8_ref[0])
        a8_re, a8_im = _cexp(CHUNK_T * l8_re_ref[0] * dt8, CHUNK_T * l8_im_ref[0] * dt8)
        nc = rp // streams_p

        def body(c, carry):
            new = []
            for b in range(streams_p):
                sr, si = carry[2 * b], carry[2 * b + 1]
                r = b * nc + c
                xr_ref[pl.ds(r, gb, stride=rp), :] = sr
                xi_ref[pl.ds(r, gb, stride=rp), :] = si
                nr, ni = _cmul(a8_re, a8_im, sr, si)
                new.append(nr + wr_ref[pl.ds(r, gb, stride=rp), :])
                new.append(ni + wi_ref[pl.ds(r, gb, stride=rp), :])
            return tuple(new)

        zero = jnp.zeros((gb, p), F32)
        fin = lax.fori_loop(0, nc, body, (zero,) * (2 * streams_p), unroll=4)
        for b in range(streams_p):
            lastp_re_ref[b] = fin[2 * b]
            lastp_im_ref[b] = fin[2 * b + 1]

        for g in range(gb):
            xr = xr_ref[g * rp:(g + 1) * rp, :].astype(BF16)
            xi = xi_ref[g * rp:(g + 1) * rp, :].astype(BF16)
            cp_ref[g] = _gelu_tanh(cp_ref[g] + _dot(xr, vre_ref[g]) + _dot(xi, vim_ref[g]))

        _from_chunk_layout(cp_ref, yp_ref)
        _from_chunk_layout(cs_ref, ys_ref)


def _ssm(up, us, x0_re, x0_im, lam_re, lam_im, log_dt, bt_re, bt_im, ct_re, ct_im, d_t, *, streams_p):
    tp, dssm = up.shape
    ts = us.shape[0]
    g = dssm // SSM_GROUP
    rp, rs = tp // CHUNK_T, ts // CHUNK_T
    nb = x0_re.shape[1]
    p = SSM_STATE
    gb = GROUPS_PER_STEP
    ngb = g // gb
    per_group = lambda *shape: pl.BlockSpec((1,) + shape, lambda i, j: (i * gb + j, 0, 0))
    per_block = lambda *shape: pl.BlockSpec((1,) + shape, lambda i, j: (i, 0, 0))
    tokens = lambda rows, **kw: pl.BlockSpec((rows, LANES), lambda i, j: (0, i), **kw)
    return pl.pallas_call(
        functools.partial(_ssm_kernel, streams_p=streams_p),
        grid=(ngb, gb),
        in_specs=[
            tokens(tp, pipeline_mode=pl.Buffered(1)), tokens(ts, pipeline_mode=pl.Buffered(1)),
            per_group(nb, p), per_group(nb, p),
            per_group(1, p), per_group(1, p), per_group(p, 1), per_group(p, 1), per_group(1, 1),
            per_block(gb, p), per_block(gb, p), per_block(gb, 1),
            per_group(LANES, p), per_group(LANES, p), per_group(p, LANES), per_group(p, LANES),
            per_group(1, LANES),
        ],
        out_specs=[
            tokens(tp), tokens(ts),
            pl.BlockSpec((streams_p, gb, p), lambda i, j: (0, i, 0)),
            pl.BlockSpec((streams_p, gb, p), lambda i, j: (0, i, 0)),
            per_group(nb, p), per_group(nb, p),
        ],
        out_shape=[
            jax.ShapeDtypeStruct((tp, dssm), F32),
            jax.ShapeDtypeStruct((ts, dssm), F32),
            jax.ShapeDtypeStruct((streams_p, g, p), F32),
            jax.ShapeDtypeStruct((streams_p, g, p), F32),
            jax.ShapeDtypeStruct((g, nb, p), F32),
            jax.ShapeDtypeStruct((g, nb, p), F32),
        ],
        scratch_shapes=[
            pltpu.VMEM((gb, rp, LANES), F32), pltpu.VMEM((gb, rs, LANES), F32),
            pltpu.VMEM((gb * rp, p), F32), pltpu.VMEM((gb * rp, p), F32),
            pltpu.VMEM((gb * rp, p), F32), pltpu.VMEM((gb * rp, p), F32),
            pltpu.VMEM((rs, p), F32), pltpu.VMEM((rs, p), F32),
            pltpu.VMEM((rs, p), F32), pltpu.VMEM((rs, p), F32),
            pltpu.VMEM((gb, p, LANES), BF16), pltpu.VMEM((gb, p, LANES), BF16),
        ],
        compiler_params=_cparams(("parallel", "arbitrary")),
        name="ssm",
    )(up, us, x0_re, x0_im,
      lam_re.reshape(g, 1, p), lam_im.reshape(g, 1, p), lam_re.reshape(g, p, 1), lam_im.reshape(g, p, 1),
      log_dt.reshape(g, 1, 1),
      lam_re.reshape(ngb, gb, p), lam_im.reshape(ngb, gb, p), log_dt.reshape(ngb, gb, 1),
      bt_re, bt_im, ct_re, ct_im, d_t)


def _glu_kernel(y_ref, w_ref, o_ref):
    y = y_ref[...]
    o_ref[...] = y * jax.nn.sigmoid(_dot(y.astype(BF16), w_ref[...]))


def _glu(y, w, *, tm):
    t, d = y.shape
    tm = min(tm, t)
    return pl.pallas_call(
        _glu_kernel,
        grid=(t // tm,),
        in_specs=[pl.BlockSpec((tm, d), lambda i: (i, 0)), pl.BlockSpec((d, d), lambda i: (0, 0))],
        out_specs=pl.BlockSpec((tm, d), lambda i: (i, 0)),
        out_shape=jax.ShapeDtypeStruct((t, d), F32),
        compiler_params=_cparams(("parallel",)),
        name="glu",
    )(y, w)


def _merge_kernel(o_ref, y_ref, x_ref, ga_ref, gs_ref, w_ref, out_ref, a_ref):
    da = o_ref.shape[1]

    @pl.when(pl.program_id(1) == 0)
    def _():
        a_ref[:, :da] = _rms(o_ref[...], ga_ref[...]).astype(BF16)
        a_ref[:, da:] = _rms(y_ref[...], gs_ref[...]).astype(BF16)

    out_ref[...] = x_ref[...] + _dot(a_ref[...], w_ref[...])


def _merge(o, y, x, g_attn, g_ssm, w_out, *, tm, tn):
    t, da = o.shape
    ds = y.shape[1]
    d = x.shape[1]
    tm, tn = min(tm, t), min(tn, d)
    return pl.pallas_call(
        _merge_kernel,
        grid=(t // tm, d // tn),
        in_specs=[
            pl.BlockSpec((tm, da), lambda i, n: (i, 0), pipeline_mode=pl.Buffered(1)),
            pl.BlockSpec((tm, ds), lambda i, n: (i, 0), pipeline_mode=pl.Buffered(1)),
            pl.BlockSpec((tm, tn), lambda i, n: (i, n)),
            pl.BlockSpec((1, da), lambda i, n: (0, 0)),
            pl.BlockSpec((1, ds), lambda i, n: (0, 0)),
            pl.BlockSpec((da + ds, tn), lambda i, n: (0, n)),
        ],
        out_specs=pl.BlockSpec((tm, tn), lambda i, n: (i, n)),
        out_shape=jax.ShapeDtypeStruct((t, d), F32),
        scratch_shapes=[pltpu.VMEM((tm, da + ds), BF16)],
        compiler_params=_cparams(("parallel", "arbitrary")),
        name="merge",
    )(o, y, x, g_attn.reshape(1, da), g_ssm.reshape(1, ds), w_out)


TILES = dict(
    ffn=dict(tm=1024, tf=256, tm_down=512, tn=512),
    split_w_in=dict(cb=256),
    inproj=dict(tm=512, tn=512),
    attn_prompt=dict(tq=512, heads=4),
    cumsum=dict(bs=512),
    glu=dict(tm=512),
    merge=dict(tm=512, tn=1024),
)


def kernel(x_prompt, x_sample, cache_k, cache_v, cache_logf, state_ssm_re, state_ssm_im, norm_ffn1, w1_a, w3_a, w2_a, norm_mix, w_in, b_f, q_norm, k_norm, lam_re, lam_im, log_dt, b_re, b_im, c_re, c_im, d_skip, w_glu, out_norm_attn, out_norm_ssm, w_out, norm_ffn2, w1_b, w3_b, w2_b):
    depth = norm_ffn1.shape[0]
    assert depth == 1
    l = 0
    bp, sp, d = x_prompt.shape
    bs, ts, _ = x_sample.shape
    past = cache_k.shape[2]
    nh = b_f.shape[1]
    da = nh * HEAD_DIM
    groups = lam_re.shape[1]
    dssm = groups * SSM_GROUP
    p = SSM_STATE

    bf = lambda w: w.astype(BF16)
    w_qkv, w_f, w_u = _split_w_in(w_in[l], da, nh, **TILES["split_w_in"])

    xp = x_prompt.reshape(bp * sp, d)
    xs = x_sample.reshape(bs * ts, d)

    ffn_a = functools.partial(_ffn, g=norm_ffn1[l], w1=bf(w1_a[l]), w3=bf(w3_a[l]), w2=bf(w2_a[l]), **TILES["ffn"])
    xp = ffn_a(xp)
    xs = ffn_a(xs)

    proj = functools.partial(_inproj, g=norm_mix[l], w_qkv=w_qkv, w_u=w_u, w_f=w_f, b_f=b_f[l],
                             q_norm=q_norm[l], k_norm=k_norm[l], **TILES["inproj"])
    qp, kp, kbp, vp, vbp, up, lfp = proj(xp)
    qs, ks, kbs, vs, vbs, us, lfs = proj(xs)

    lfp_t = lfp.reshape(bp, sp, nh).transpose(0, 2, 1).reshape(bp * nh, sp)
    cp_t = _cumsum_rows(lfp_t, **TILES["cumsum"]).reshape(bp, nh, sp)
    o_p = _attn_prompt(qp.reshape(bp, sp, da), kbp.reshape(bp, sp, da), vbp.reshape(bp, sp, da), cp_t,
                       **TILES["attn_prompt"])

    lfs_all = jnp.concatenate([cache_logf[l].astype(F32), lfs.reshape(bs, ts, nh)], axis=1)
    lfs_t = lfs_all.transpose(0, 2, 1).reshape(bs * nh, past + ts)
    cs_t = _cumsum_rows(lfs_t, bs=past + ts).reshape(bs, nh, past + ts)
    cs_col = cs_t[:, :, past:].transpose(0, 2, 1)
    o_s = _attn_sample(qs.reshape(bs, ts, da),
                       cache_k[l].reshape(bs, past * nh, HEAD_DIM), cache_v[l].reshape(bs, past * nh, HEAD_DIM),
                       kbs.reshape(bs, ts, da), vbs.reshape(bs, ts, da), cs_t, cs_col)

    bt = lambda b: jnp.tile(b.transpose(0, 2, 1), (1, CHUNK_T, 1))
    ct = lambda c: jnp.tile(c.transpose(0, 2, 1), (1, 1, CHUNK_T))
    d_t = jnp.tile(d_skip[l].reshape(groups, 1, SSM_GROUP), (1, 1, CHUNK_T))
    yp, ys, lp_re, lp_im, ls_re, ls_im = _ssm(
        up, us,
        state_ssm_re[l].astype(F32).transpose(1, 0, 2), state_ssm_im[l].astype(F32).transpose(1, 0, 2),
        lam_re[l], lam_im[l], log_dt[l], bt(b_re[l]), bt(b_im[l]), ct(c_re[l]), ct(c_im[l]), d_t,
        streams_p=bp)

    glu = functools.partial(_glu, w=bf(w_glu[l]), **TILES["glu"])
    merge = functools.partial(_merge, g_attn=out_norm_attn[l], g_ssm=out_norm_ssm[l], w_out=bf(w_out[l]),
                              **TILES["merge"])
    xp = merge(o_p.reshape(bp * sp, da), glu(yp), xp)
    xs = merge(o_s.reshape(bs * ts, da), glu(ys), xs)

    ffn_b = functools.partial(_ffn, g=norm_ffn2[l], w1=bf(w1_b[l]), w3=bf(w3_b[l]), w2=bf(w2_b[l]), **TILES["ffn"])
    xp = ffn_b(xp)
    xs = ffn_b(xs)

    return (xp.reshape(bp, sp, d), xs.reshape(bs, ts, d),
            kp.reshape(1, bp, sp, nh, HEAD_DIM), vp.reshape(1, bp, sp, nh, HEAD_DIM), lfp.reshape(1, bp, sp, nh),
            lp_re[None], lp_im[None],
            ks.reshape(1, bs, ts, nh, HEAD_DIM), vs.reshape(1, bs, ts, nh, HEAD_DIM), lfs.reshape(1, bs, ts, nh),
            ls_re.transpose(1, 0, 2)[None], ls_im.transpose(1, 0, 2)[None])
```

```python
import functools
import math

import jax
import jax.numpy as jnp
import numpy as np
from jax import lax
from jax.experimental import pallas as pl
from jax.experimental.pallas import tpu as pltpu

F32 = jnp.float32
BF16 = jnp.bfloat16

EPS = 1e-6
NEG_INF = -1e30
LOG2E = math.log2(math.e)
HEAD_DIM = 128
SSM_GROUP = 16
SSM_STATE = 64
LANES = 128
MXU_COLS = 256
CHUNK_T = LANES // SSM_GROUP
GROUPS_PER_STEP = 8
VMEM_LIMIT = 56 * 1024 * 1024


def _cparams(sem):
    return pltpu.CompilerParams(dimension_semantics=sem, vmem_limit_bytes=VMEM_LIMIT)


def _rms(x, g):
    ms = jnp.mean(x * x, axis=-1, keepdims=True)
    return x * lax.rsqrt(ms + EPS) * g


def _dot(a, b):
    return jnp.dot(a, b, preferred_element_type=F32)


def _dot_nt(a, b):
    return lax.dot_general(a, b, (((1,), (1,)), ((), ())), preferred_element_type=F32)


def _ffn_up_kernel(x_ref, g_ref, w1_ref, w3_ref, act_ref, h_ref):
    @pl.when(pl.program_id(1) == 0)
    def _():
        h_ref[...] = _rms(x_ref[...], g_ref[...]).astype(BF16)

    h = h_ref[...]
    a = _dot(h, w1_ref[...])
    b = _dot(h, w3_ref[...])
    act_ref[...] = (0.5 * a * jax.nn.sigmoid(a) * b).astype(BF16)


def _ffn_down_kernel(act_ref, w2_ref, x_ref, o_ref):
    o_ref[...] = x_ref[...] + _dot(act_ref[...], w2_ref[...])


def _ffn(x, g, w1, w3, w2, *, tm, tf, tm_down, tn):
    t, d = x.shape
    f = w1.shape[1]
    tm, tf, tn = min(tm, t), min(tf, f), min(tn, d)
    once = pl.Buffered(1)
    act = pl.pallas_call(
        _ffn_up_kernel,
        grid=(t // tm, f // tf),
        in_specs=[
            pl.BlockSpec((tm, d), lambda i, j: (i, 0), pipeline_mode=once),
            pl.BlockSpec((1, d), lambda i, j: (0, 0)),
            pl.BlockSpec((d, tf), lambda i, j: (0, j)),
            pl.BlockSpec((d, tf), lambda i, j: (0, j)),
        ],
        out_specs=pl.BlockSpec((tm, tf), lambda i, j: (i, j)),
        out_shape=jax.ShapeDtypeStruct((t, f), BF16),
        scratch_shapes=[pltpu.VMEM((tm, d), BF16)],
        compiler_params=_cparams(("parallel", "arbitrary")),
        name="ffn_up",
    )(x, g.reshape(1, d), w1, w3)
    tm = min(tm_down, t)
    return pl.pallas_call(
        _ffn_down_kernel,
        grid=(t // tm, d // tn),
        in_specs=[
            pl.BlockSpec((tm, f), lambda i, n: (i, 0)),
            pl.BlockSpec((f, tn), lambda i, n: (0, n)),
            pl.BlockSpec((tm, tn), lambda i, n: (i, n)),
        ],
        out_specs=pl.BlockSpec((tm, tn), lambda i, n: (i, n)),
        out_shape=jax.ShapeDtypeStruct((t, d), F32),
        compiler_params=_cparams(("parallel", "arbitrary")),
        name="ffn_down",
    )(act, w2, x)


def _inproj_kernel(x_ref, g_ref, w_ref, wu_ref, wf_ref, bf_ref, qn_ref, kn_ref,
                   q_ref, k_ref, kb_ref, v_ref, vb_ref, u_ref, lf_ref, h_ref, *, tiles):
    n = pl.program_id(1)

    @pl.when(n == 0)
    def _():
        h = _rms(x_ref[...], g_ref[...]).astype(BF16)
        h_ref[...] = h
        zf = _dot_nt(h, wf_ref[...]) + bf_ref[...]
        lf_ref[...] = jnp.minimum(zf, 0.0) - jnp.log1p(jnp.exp(-jnp.abs(zf)))

    sec = n // tiles
    tn = w_ref.shape[0]
    cw = min(MXU_COLS, tn)

    def chunks(weights):
        for c in range(tn // cw):
            cols = slice(c * cw, (c + 1) * cw)
            yield cols, _dot_nt(h_ref[...], weights[cols, :])

    def head_norm(z, gain):
        outs = [_rms(z[:, hh * HEAD_DIM:(hh + 1) * HEAD_DIM], gain) for hh in range(cw // HEAD_DIM)]
        return outs[0] if len(outs) == 1 else jnp.concatenate(outs, axis=-1)

    @pl.when(sec == 0)
    def _():
        for cols, z in chunks(w_ref):
            q_ref[:, cols] = head_norm(z, qn_ref[...]).astype(BF16)

    @pl.when(sec == 1)
    def _():
        for cols, z in chunks(w_ref):
            kk = head_norm(z, kn_ref[...])
            k_ref[:, cols] = kk
            kb_ref[:, cols] = kk.astype(BF16)

    @pl.when(sec == 2)
    def _():
        for cols, z in chunks(w_ref):
            v_ref[:, cols] = z
            vb_ref[:, cols] = z.astype(BF16)

    @pl.when(sec >= 3)
    def _():
        for cols, z in chunks(wu_ref):
            u_ref[:, cols] = z


def _inproj(x, g, w_qkv, w_u, w_f, b_f, q_norm, k_norm, *, tm, tn):
    t, d = x.shape
    da = w_qkv.shape[0] // 3
    du = w_u.shape[0]
    nh = w_f.shape[0]
    tm, tn = min(tm, t), min(tn, da, du)
    tiles, tiles_u = da // tn, du // tn

    def sec_map(s, count):
        return lambda i, n: (i, jnp.clip(n - s * tiles, 0, count - 1))

    row = lambda i, n: (0, 0)
    attn_out = lambda dt: jax.ShapeDtypeStruct((t, da), dt)
    return pl.pallas_call(
        functools.partial(_inproj_kernel, tiles=tiles),
        grid=(t // tm, 3 * tiles + tiles_u),
        in_specs=[
            pl.BlockSpec((tm, d), lambda i, n: (i, 0), pipeline_mode=pl.Buffered(1)),
            pl.BlockSpec((1, d), row),
            pl.BlockSpec((tn, d), lambda i, n: (jnp.minimum(n, 3 * tiles - 1), 0)),
            pl.BlockSpec((tn, d), lambda i, n: (jnp.clip(n - 3 * tiles, 0, tiles_u - 1), 0)),
            pl.BlockSpec((nh, d), row),
            pl.BlockSpec((1, nh), row),
            pl.BlockSpec((1, HEAD_DIM), row),
            pl.BlockSpec((1, HEAD_DIM), row),
        ],
        out_specs=[
            pl.BlockSpec((tm, tn), sec_map(0, tiles)),
            pl.BlockSpec((tm, tn), sec_map(1, tiles)),
            pl.BlockSpec((tm, tn), sec_map(1, tiles)),
            pl.BlockSpec((tm, tn), sec_map(2, tiles)),
            pl.BlockSpec((tm, tn), sec_map(2, tiles)),
            pl.BlockSpec((tm, tn), sec_map(3, tiles_u)),
            pl.BlockSpec((tm, nh), lambda i, n: (i, 0)),
        ],
        out_shape=[attn_out(BF16), attn_out(F32), attn_out(BF16), attn_out(F32), attn_out(BF16),
                   jax.ShapeDtypeStruct((t, du), F32), jax.ShapeDtypeStruct((t, nh), F32)],
        scratch_shapes=[pltpu.VMEM((tm, d), BF16)],
        compiler_params=_cparams(("parallel", "arbitrary")),
        name="inproj",
    )(x, g.reshape(1, d), w_qkv, w_u, w_f, b_f.reshape(1, nh),
      q_norm.reshape(1, HEAD_DIM), k_norm.reshape(1, HEAD_DIM))


def _split3(x):
    hi = x.astype(BF16)
    r1 = x - hi.astype(F32)
    mid = r1.astype(BF16)
    lo = (r1 - mid.astype(F32)).astype(BF16)
    return hi, mid, lo


def _cumsum_kernel(x_ref, o_ref, carry_ref):
    @pl.when(pl.program_id(0) == 0)
    def _():
        carry_ref[...] = jnp.zeros_like(carry_ref)

    x = x_ref[...]
    bs = x.shape[1]
    row = lax.broadcasted_iota(jnp.int32, (bs, bs), 0)
    col = lax.broadcasted_iota(jnp.int32, (bs, bs), 1)
    tri = jnp.where(row <= col, 1.0, 0.0).astype(BF16)
    hi, mid, lo = _split3(x)
    c = _dot(hi, tri) + _dot(mid, tri) + _dot(lo, tri) + carry_ref[...]
    o_ref[...] = c
    carry_ref[...] = c[:, bs - 1:bs]


def _cumsum_rows(x, *, bs):
    r, s = x.shape
    bs = min(bs, s)
    return pl.pallas_call(
        _cumsum_kernel,
        grid=(s // bs,),
        in_specs=[pl.BlockSpec((r, bs), lambda j: (0, j))],
        out_specs=pl.BlockSpec((r, bs), lambda j: (0, j)),
        out_shape=jax.ShapeDtypeStruct((r, s), F32),
        scratch_shapes=[pltpu.VMEM((r, 1), F32)],
        compiler_params=_cparams(("arbitrary",)),
        name="cumsum",
    )(x)


def _lane_tile(x, width):
    return x if width == LANES else jnp.tile(x, (1, width // LANES))


def _attn_prompt_kernel(qi_tab, ki_tab, q_ref, k_ref, v_ref, ccol_ref, crow_ref, o_ref,
                        m_ref, l_ref, acc_ref, cq_ref, *, heads, scale):
    t = pl.program_id(2)
    qi, ki = qi_tab[t], ki_tab[t]
    tq, tk = q_ref.shape[1], k_ref.shape[1]

    @pl.when(ki == 0)
    def _():
        m_ref[...] = jnp.full_like(m_ref, NEG_INF)
        l_ref[...] = jnp.zeros_like(l_ref)
        acc_ref[...] = jnp.zeros_like(acc_ref)
        for hh in range(heads):
            cq_ref[hh] = jnp.broadcast_to(ccol_ref[0, hh] * LOG2E, (tq, LANES))

    def step(masked):
        for hh in range(heads):
            sl = slice(hh * HEAD_DIM, (hh + 1) * HEAD_DIM)
            s = _dot_nt(q_ref[0, :, sl], k_ref[0, :, sl]) * (scale * LOG2E)
            s = s + _lane_tile(cq_ref[hh], tk) - crow_ref[0, hh] * LOG2E
            if masked:
                row = lax.broadcasted_iota(jnp.int32, (tq, tk), 0)
                col = lax.broadcasted_iota(jnp.int32, (tq, tk), 1)
                s = jnp.where(col <= row, s, NEG_INF)
            m_prev = m_ref[hh]
            m_new = jnp.maximum(m_prev, jnp.max(s, axis=-1, keepdims=True))
            alpha = jnp.exp2(m_prev - m_new)
            p = jnp.exp2(s - _lane_tile(m_new, tk))
            l_ref[hh] = alpha * l_ref[hh] + jnp.sum(p, axis=-1, keepdims=True)
            acc_ref[hh] = alpha * acc_ref[hh] + _dot(p.astype(BF16), v_ref[0, :, sl])
            m_ref[hh] = m_new

    @pl.when(ki < qi)
    def _():
        step(False)

    @pl.when(ki == qi)
    def _():
        step(True)
        for hh in range(heads):
            o_ref[0, :, hh * HEAD_DIM:(hh + 1) * HEAD_DIM] = acc_ref[hh] / l_ref[hh]


def _attn_prompt(q, kb, vb, c_t, *, tq, heads):
    b, s, da = q.shape
    nh = da // HEAD_DIM
    tq = min(tq, s)
    heads = min(heads, nh)
    nq = s // tq
    pairs = [(i, j) for i in range(nq) for j in range(i + 1)]
    qi_tab = jnp.asarray([p[0] for p in pairs], jnp.int32)
    ki_tab = jnp.asarray([p[1] for p in pairs], jnp.int32)
    w = heads * HEAD_DIM
    grid_spec = pltpu.PrefetchScalarGridSpec(
        num_scalar_prefetch=2,
        grid=(b, nh // heads, len(pairs)),
        in_specs=[
            pl.BlockSpec((1, tq, w), lambda bi, hg, t, qt, kt: (bi, qt[t], hg)),
            pl.BlockSpec((1, tq, w), lambda bi, hg, t, qt, kt: (bi, kt[t], hg)),
            pl.BlockSpec((1, tq, w), lambda bi, hg, t, qt, kt: (bi, kt[t], hg)),
            pl.BlockSpec((1, heads, tq, 1), lambda bi, hg, t, qt, kt: (bi, hg, qt[t], 0)),
            pl.BlockSpec((1, heads, 1, tq), lambda bi, hg, t, qt, kt: (bi, hg, 0, kt[t])),
        ],
        out_specs=pl.BlockSpec((1, tq, w), lambda bi, hg, t, qt, kt: (bi, qt[t], hg)),
        scratch_shapes=[
            pltpu.VMEM((heads, tq, LANES), F32),
            pltpu.VMEM((heads, tq, LANES), F32),
            pltpu.VMEM((heads, tq, HEAD_DIM), F32),
            pltpu.VMEM((heads, tq, LANES), F32),
        ],
    )
    return pl.pallas_call(
        functools.partial(_attn_prompt_kernel, heads=heads, scale=HEAD_DIM ** -0.5),
        grid_spec=grid_spec,
        out_shape=jax.ShapeDtypeStruct((b, s, da), F32),
        compiler_params=_cparams(("parallel", "parallel", "arbitrary")),
        name="attn_prompt",
    )(qi_tab, ki_tab, q, kb, vb, c_t.reshape(b, nh, s, 1), c_t.reshape(b, nh, 1, s))


def _attn_sample_kernel(q_ref, ck_ref, cv_ref, kn_ref, vn_ref, crow_ref, ccol_ref, o_ref, kq_ref, vq_ref, *, scale):
    t = q_ref.shape[1]
    nh = q_ref.shape[2] // HEAD_DIM
    past = ck_ref.shape[1] // nh
    row = lax.broadcasted_iota(jnp.int32, (t, t), 0)
    col = lax.broadcasted_iota(jnp.int32, (t, t), 1)

    s1 = kq_ref.shape[0]
    s2 = nh // s1
    for r in range(s1):
        kq_ref[r] = ck_ref[0, pl.ds(r, past * s2, stride=s1), :]
        vq_ref[r] = cv_ref[0, pl.ds(r, past * s2, stride=s1), :]

    for hh in range(nh):
        sl = slice(hh * HEAD_DIM, (hh + 1) * HEAD_DIM)
        r, a = hh % s1, hh // s1
        cached = pl.ds(a, past, stride=s2) if s2 > 1 else slice(None)
        qh = q_ref[0, :, sl]
        cq = ccol_ref[0, :, hh:hh + 1]
        ck = crow_ref[0, hh:hh + 1, :]
        sc = _dot_nt(qh, kq_ref[r, cached, :].astype(BF16)) * scale + cq - ck[:, :past]
        sn = _dot_nt(qh, kn_ref[0, :, sl]) * scale + cq - ck[:, past:]
        sn = jnp.where(col <= row, sn, NEG_INF)
        m = jnp.maximum(jnp.max(sc, axis=-1, keepdims=True), jnp.max(sn, axis=-1, keepdims=True))
        pc = jnp.exp(sc - m)
        pn = jnp.exp(sn - m)
        l = jnp.sum(pc, axis=-1, keepdims=True) + jnp.sum(pn, axis=-1, keepdims=True)
        o = _dot(pc.astype(BF16), vq_ref[r, cached, :].astype(BF16)) + _dot(pn.astype(BF16), vn_ref[0, :, sl])
        o_ref[0, :, sl] = o / l


def _attn_sample(q, cache_k, cache_v, kb, vb, c_t, c_col):
    b, t, da = q.shape
    nh = da // HEAD_DIM
    rows = cache_k.shape[1]
    first_stride = math.gcd(nh, 4)
    blk = lambda *shape: pl.BlockSpec((1,) + shape, lambda i: (i, 0, 0))
    return pl.pallas_call(
        functools.partial(_attn_sample_kernel, scale=HEAD_DIM ** -0.5),
        grid=(b,),
        in_specs=[blk(t, da), blk(rows, HEAD_DIM), blk(rows, HEAD_DIM), blk(t, da), blk(t, da),
                  blk(nh, rows // nh + t), blk(t, nh)],
        out_specs=blk(t, da),
        out_shape=jax.ShapeDtypeStruct((b, t, da), F32),
        scratch_shapes=[pltpu.VMEM((first_stride, rows // first_stride, HEAD_DIM), F32)] * 2,
        compiler_params=_cparams(("parallel",)),
        name="attn_sample",
    )(q, cache_k, cache_v, kb, vb, c_t, c_col)


def _cexp(mag_log, ang):
    mag = jnp.exp(mag_log)
    return mag * jnp.cos(ang), mag * jnp.sin(ang)


def _cmul(ar, ai, br, bi):
    return ar * br - ai * bi, ar * bi + ai * br


def _cpow_table(zr, zi, k):
    acc_r = jnp.broadcast_to(zr, k.shape)
    acc_i = jnp.broadcast_to(zi, k.shape)
    bit = 1
    while bit < CHUNK_T:
        take = (k & bit) != 0
        acc_r, acc_i = _cmul(acc_r, acc_i, jnp.where(take, zr, 1.0), jnp.where(take, zi, 0.0))
        zr, zi = _cmul(zr, zi, zr, zi)
        bit *= 2
    return acc_r, acc_i


def _gelu_tanh(x):
    return 0.5 * x * (1.0 + jnp.tanh(math.sqrt(2.0 / math.pi) * (x + 0.044715 * (x * x * x))))


def _split2(x):
    hi = x.astype(BF16)
    lo = (x - hi.astype(F32)).astype(BF16)
    return hi, lo


def _lane_block(rows):
    return lax.broadcasted_iota(jnp.int32, (rows, LANES), 1) // SSM_GROUP


def _block_transpose(xs):
    blk = _lane_block(xs[0].shape[0])
    s = CHUNK_T // 2
    while s:
        upper = (blk & s) != 0
        nxt = list(xs)
        for a in range(CHUNK_T):
            if a & s:
                continue
            lo, hi = xs[a], xs[a + s]
            nxt[a] = jnp.where(upper, pltpu.roll(hi, s * SSM_GROUP, axis=1), lo)
            nxt[a + s] = jnp.where(upper, hi, pltpu.roll(lo, (CHUNK_T - s) * SSM_GROUP, axis=1))
        xs = nxt
        s //= 2
    return xs


def _to_chunk_layout(src_ref, dst_ref):
    rows = dst_ref.shape[1]
    toks = [src_ref[pl.ds(t, rows, stride=CHUNK_T), :] for t in range(CHUNK_T)]
    for i, x in enumerate(_block_transpose(toks)):
        dst_ref[i] = x


def _from_chunk_layout(src_ref, dst_ref):
    rows = src_ref.shape[1]
    for t, x in enumerate(_block_transpose([src_ref[i] for i in range(CHUNK_T)])):
        dst_ref[pl.ds(t, rows, stride=CHUNK_T), :] = x


def _ssm_kernel(up_ref, us_ref, x0r_ref, x0i_ref, lrow_re_ref, lrow_im_ref, lcol_re_ref, lcol_im_ref,
                ldt_ref, l8_re_ref, l8_im_ref, ldt8_ref, bt_re_ref, bt_im_ref, ct_re_ref, ct_im_ref, d_ref,
                yp_ref, ys_ref, lastp_re_ref, lastp_im_ref, lasts_re_ref, lasts_im_ref,
                cp_ref, cs_ref, wr_ref, wi_ref, xr_ref, xi_ref, wsr_ref, wsi_ref, xsr_ref, xsi_ref,
                vre_ref, vim_ref, *, streams_p):
    j = pl.program_id(1)
    gb = GROUPS_PER_STEP
    rp = cp_ref.shape[1]
    p = SSM_STATE

    @pl.when(j == 0)
    def _():
        _to_chunk_layout(up_ref, cp_ref)
        _to_chunk_layout(us_ref, cs_ref)

    dt = jnp.exp(ldt_ref[0])

    a_row = lrow_re_ref[0] * dt
    w_row = lrow_im_ref[0] * dt
    s_idx = lax.broadcasted_iota(jnp.int32, (LANES, p), 0) // SSM_GROUP
    en_re, en_im = _cpow_table(*_cexp(-a_row, -w_row), s_idx)
    l8_re, l8_im = _cexp(CHUNK_T * a_row, CHUNK_T * w_row)
    lb_re, lb_im = _cexp(a_row, w_row)
    lr, li = lrow_re_ref[0], lrow_im_ref[0]
    inv = 1.0 / (lr * lr + li * li)
    cf_re, cf_im = _cmul(lb_re - 1.0, lb_im, lr * inv, -li * inv)
    bb_re, bb_im = _cmul(cf_re, cf_im, bt_re_ref[0], bt_im_ref[0])
    f_re, f_im = _cmul(en_re, en_im, bb_re, bb_im)
    e7_re, e7_im = _cmul(en_re, en_im, l8_re, l8_im)
    w_re, w_im = _cmul(e7_re, e7_im, bb_re, bb_im)
    w_re, w_im = w_re.astype(BF16), w_im.astype(BF16)

    a_col = lcol_re_ref[0] * dt
    w_col = lcol_im_ref[0] * dt
    t_idx = lax.broadcasted_iota(jnp.int32, (p, LANES), 1) // SSM_GROUP
    et_re, et_im = _cpow_table(*_cexp(a_col, w_col), t_idx)
    g_re, g_im = _cmul(et_re, et_im, ct_re_ref[0], ct_im_ref[0])
    v_re, v_im = g_re.astype(BF16), (-g_im).astype(BF16)
    vre_ref[j] = v_re
    vim_ref[j] = v_im

    fs = jnp.concatenate([f_re, -f_im], axis=1)
    hs = jnp.concatenate([g_re, g_im], axis=0)
    fh, fl = _split2(fs)
    hh, hl = _split2(hs)
    tmat = _dot(fh, hh) + _dot(fh, hl) + _dot(fl, hh)
    srow = lax.broadcasted_iota(jnp.int32, (LANES, LANES), 0) // SSM_GROUP
    tcol = lax.broadcasted_iota(jnp.int32, (LANES, LANES), 1) // SSM_GROUP
    tmat = jnp.where(srow <= tcol, tmat, 0.0).astype(BF16)

    d_row = d_ref[0]

    u = cp_ref[j]
    ub = u.astype(BF16)
    cp_ref[j] = _dot(ub, tmat) + d_row * u
    base = pl.multiple_of(j * rp, 8)
    wr_ref[pl.ds(base, rp), :] = _dot(ub, w_re)
    wi_ref[pl.ds(base, rp), :] = _dot(ub, w_im)

    us = cs_ref[j]
    usb = us.astype(BF16)
    wsr_ref[...] = _dot(usb, w_re)
    wsi_ref[...] = _dot(usb, w_im)
    nb = x0r_ref.shape[1]
    ncs = us.shape[0] // nb
    sr, si = x0r_ref[0], x0i_ref[0]
    for c in range(ncs):
        rows = pl.ds(c, nb, stride=ncs)
        xsr_ref[rows, :] = sr
        xsi_ref[rows, :] = si
        nr, ni = _cmul(l8_re, l8_im, sr, si)
        sr = nr + wsr_ref[rows, :]
        si = ni + wsi_ref[rows, :]
    lasts_re_ref[0] = sr
    lasts_im_ref[0] = si
    xs_re = xsr_ref[...].astype(BF16)
    xs_im = xsi_ref[...].astype(BF16)
    cs_ref[j] = _gelu_tanh(_dot(usb, tmat) + _dot(xs_re, v_re) + _dot(xs_im, v_im) + d_row * us)

    @pl.when(j == gb - 1)
    def _():
        dt8 = jnp.exp(ldt8_ref[0])
        a8_re, a8_im = _cexp(CHUNK_T * l8_re_ref[0] * dt8, CHUNK_T * l8_im_ref[0] * dt8)
        nc = rp // streams_p

        def body(c, carry):
            new = []
            for b in range(streams_p):
                sr, si = carry[2 * b], carry[2 * b + 1]
                r = b * nc + c
                xr_ref[pl.ds(r, gb, stride=rp), :] = sr
                xi_ref[pl.ds(r, gb, stride=rp), :] = si
                nr, ni = _cmul(a8_re, a8_im, sr, si)
                new.append(nr + wr_ref[pl.ds(r, gb, stride=rp), :])
                new.append(ni + wi_ref[pl.ds(r, gb, stride=rp), :])
            return tuple(new)

        zero = jnp.zeros((gb, p), F32)
        fin = lax.fori_loop(0, nc, body, (zero,) * (2 * streams_p), unroll=4)
        for b in range(streams_p):
            lastp_re_ref[b] = fin[2 * b]
            lastp_im_ref[b] = fin[2 * b + 1]

        for g in range(gb):
            xr = xr_ref[g * rp:(g + 1) * rp, :].astype(BF16)
            xi = xi_ref[g * rp:(g + 1) * rp, :].astype(BF16)
            cp_ref[g] = _gelu_tanh(cp_ref[g] + _dot(xr, vre_ref[g]) + _dot(xi, vim_ref[g]))

        _from_chunk_layout(cp_ref, yp_ref)
        _from_chunk_layout(cs_ref, ys_ref)


def _ssm(up, us, x0_re, x0_im, lam_re, lam_im, log_dt, bt_re, bt_im, ct_re, ct_im, d_t, *, streams_p):
    tp, dssm = up.shape
    ts = us.shape[0]
    g = dssm // SSM_GROUP
    rp, rs = tp // CHUNK_T, ts // CHUNK_T
    nb = x0_re.shape[1]
    p = SSM_STATE
    gb = GROUPS_PER_STEP
    ngb = g // gb
    per_group = lambda *shape: pl.BlockSpec((1,) + shape, lambda i, j: (i * gb + j, 0, 0))
    per_block = lambda *shape: pl.BlockSpec((1,) + shape, lambda i, j: (i, 0, 0))
    tokens = lambda rows, **kw: pl.BlockSpec((rows, LANES), lambda i, j: (0, i), **kw)
    return pl.pallas_call(
        functools.partial(_ssm_kernel, streams_p=streams_p),
        grid=(ngb, gb),
        in_specs=[
            tokens(tp, pipeline_mode=pl.Buffered(1)), tokens(ts, pipeline_mode=pl.Buffered(1)),
            per_group(nb, p), per_group(nb, p),
            per_group(1, p), per_group(1, p), per_group(p, 1), per_group(p, 1), per_group(1, 1),
            per_block(gb, p), per_block(gb, p), per_block(gb, 1),
            per_group(LANES, p), per_group(LANES, p), per_group(p, LANES), per_group(p, LANES),
            per_group(1, LANES),
        ],
        out_specs=[
            tokens(tp), tokens(ts),
            pl.BlockSpec((streams_p, gb, p), lambda i, j: (0, i, 0)),
            pl.BlockSpec((streams_p, gb, p), lambda i, j: (0, i, 0)),
            per_group(nb, p), per_group(nb, p),
        ],
        out_shape=[
            jax.ShapeDtypeStruct((tp, dssm), F32),
            jax.ShapeDtypeStruct((ts, dssm), F32),
            jax.ShapeDtypeStruct((streams_p, g, p), F32),
            jax.ShapeDtypeStruct((streams_p, g, p), F32),
            jax.ShapeDtypeStruct((g, nb, p), F32),
            jax.ShapeDtypeStruct((g, nb, p), F32),
        ],
        scratch_shapes=[
            pltpu.VMEM((gb, rp, LANES), F32), pltpu.VMEM((gb, rs, LANES), F32),
            pltpu.VMEM((gb * rp, p), F32), pltpu.VMEM((gb * rp, p), F32),
            pltpu.VMEM((gb * rp, p), F32), pltpu.VMEM((gb * rp, p), F32),
            pltpu.VMEM((rs, p), F32), pltpu.VMEM((rs, p), F32),
            pltpu.VMEM((rs, p), F32), pltpu.VMEM((rs, p), F32),
            pltpu.VMEM((gb, p, LANES), BF16), pltpu.VMEM((gb, p, LANES), BF16),
        ],
        compiler_params=_cparams(("parallel", "arbitrary")),
        name="ssm",
    )(up, us, x0_re, x0_im,
      lam_re.reshape(g, 1, p), lam_im.reshape(g, 1, p), lam_re.reshape(g, p, 1), lam_im.reshape(g, p, 1),
      log_dt.reshape(g, 1, 1),
      lam_re.reshape(ngb, gb, p), lam_im.reshape(ngb, gb, p), log_dt.reshape(ngb, gb, 1),
      bt_re, bt_im, ct_re, ct_im, d_t)


def _glu_kernel(y_ref, w_ref, o_ref):
    y = y_ref[...]
    o_ref[...] = y * jax.nn.sigmoid(_dot(y.astype(BF16), w_ref[...]))


def _glu(y, w, *, tm):
    t, d = y.shape
    tm = min(tm, t)
    return pl.pallas_call(
        _glu_kernel,
        grid=(t // tm,),
        in_specs=[pl.BlockSpec((tm, d), lambda i: (i, 0)), pl.BlockSpec((d, d), lambda i: (0, 0))],
        out_specs=pl.BlockSpec((tm, d), lambda i: (i, 0)),
        out_shape=jax.ShapeDtypeStruct((t, d), F32),
        compiler_params=_cparams(("parallel",)),
        name="glu",
    )(y, w)


def _merge_kernel(o_ref, y_ref, x_ref, ga_ref, gs_ref, w_ref, out_ref, a_ref):
    da = o_ref.shape[1]

    @pl.when(pl.program_id(1) == 0)
    def _():
        a_ref[:, :da] = _rms(o_ref[...], ga_ref[...]).astype(BF16)
        a_ref[:, da:] = _rms(y_ref[...], gs_ref[...]).astype(BF16)

    out_ref[...] = x_ref[...] + _dot(a_ref[...], w_ref[...])


def _merge(o, y, x, g_attn, g_ssm, w_out, *, tm, tn):
    t, da = o.shape
    ds = y.shape[1]
    d = x.shape[1]
    tm, tn = min(tm, t), min(tn, d)
    return pl.pallas_call(
        _merge_kernel,
        grid=(t // tm, d // tn),
        in_specs=[
            pl.BlockSpec((tm, da), lambda i, n: (i, 0), pipeline_mode=pl.Buffered(1)),
            pl.BlockSpec((tm, ds), lambda i, n: (i, 0), pipeline_mode=pl.Buffered(1)),
            pl.BlockSpec((tm, tn), lambda i, n: (i, n)),
            pl.BlockSpec((1, da), lambda i, n: (0, 0)),
            pl.BlockSpec((1, ds), lambda i, n: (0, 0)),
            pl.BlockSpec((da + ds, tn), lambda i, n: (0, n)),
        ],
        out_specs=pl.BlockSpec((tm, tn), lambda i, n: (i, n)),
        out_shape=jax.ShapeDtypeStruct((t, d), F32),
        scratch_shapes=[pltpu.VMEM((tm, da + ds), BF16)],
        compiler_params=_cparams(("parallel", "arbitrary")),
        name="merge",
    )(o, y, x, g_attn.reshape(1, da), g_ssm.reshape(1, ds), w_out)


TILES = dict(
    ffn=dict(tm=1024, tf=256, tm_down=512, tn=512),
    inproj=dict(tm=512, tn=512),
    attn_prompt=dict(tq=512, heads=8),
    cumsum=dict(bs=512),
    glu=dict(tm=512),
    merge=dict(tm=1024, tn=512),
)


def kernel(x_prompt, x_sample, cache_k, cache_v, cache_logf, state_ssm_re, state_ssm_im, norm_ffn1, w1_a, w3_a, w2_a, norm_mix, w_in, b_f, q_norm, k_norm, lam_re, lam_im, log_dt, b_re, b_im, c_re, c_im, d_skip, w_glu, out_norm_attn, out_norm_ssm, w_out, norm_ffn2, w1_b, w3_b, w2_b):
    depth = norm_ffn1.shape[0]
    assert depth == 1
    l = 0
    bp, sp, d = x_prompt.shape
    bs, ts, _ = x_sample.shape
    past = cache_k.shape[2]
    nh = b_f.shape[1]
    da = nh * HEAD_DIM
    groups = lam_re.shape[1]
    dssm = groups * SSM_GROUP
    p = SSM_STATE

    bf = lambda w: w.astype(BF16)
    w_in_t = jnp.swapaxes(w_in[l], 0, 1)
    w_qkv, w_f, w_u = bf(w_in_t[:3 * da]), bf(w_in_t[3 * da:3 * da + nh]), bf(w_in_t[3 * da + nh:])

    xp = x_prompt.reshape(bp * sp, d)
    xs = x_sample.reshape(bs * ts, d)

    ffn_a = functools.partial(_ffn, g=norm_ffn1[l], w1=bf(w1_a[l]), w3=bf(w3_a[l]), w2=bf(w2_a[l]), **TILES["ffn"])
    xp = ffn_a(xp)
    xs = ffn_a(xs)

    proj = functools.partial(_inproj, g=norm_mix[l], w_qkv=w_qkv, w_u=w_u, w_f=w_f, b_f=b_f[l],
                             q_norm=q_norm[l], k_norm=k_norm[l], **TILES["inproj"])
    qp, kp, kbp, vp, vbp, up, lfp = proj(xp)
    qs, ks, kbs, vs, vbs, us, lfs = proj(xs)

    lfp_t = lfp.reshape(bp, sp, nh).transpose(0, 2, 1).reshape(bp * nh, sp)
    cp_t = _cumsum_rows(lfp_t, **TILES["cumsum"]).reshape(bp, nh, sp)
    o_p = _attn_prompt(qp.reshape(bp, sp, da), kbp.reshape(bp, sp, da), vbp.reshape(bp, sp, da), cp_t,
                       **TILES["attn_prompt"])

    lfs_all = jnp.concatenate([cache_logf[l].astype(F32), lfs.reshape(bs, ts, nh)], axis=1)
    lfs_t = lfs_all.transpose(0, 2, 1).reshape(bs * nh, past + ts)
    cs_t = _cumsum_rows(lfs_t, bs=past + ts).reshape(bs, nh, past + ts)
    cs_col = cs_t[:, :, past:].transpose(0, 2, 1)
    o_s = _attn_sample(qs.reshape(bs, ts, da),
                       cache_k[l].reshape(bs, past * nh, HEAD_DIM), cache_v[l].reshape(bs, past * nh, HEAD_DIM),
                       kbs.reshape(bs, ts, da), vbs.reshape(bs, ts, da), cs_t, cs_col)

    bt = lambda b: jnp.tile(b.transpose(0, 2, 1), (1, CHUNK_T, 1))
    ct = lambda c: jnp.tile(c.transpose(0, 2, 1), (1, 1, CHUNK_T))
    d_t = jnp.tile(d_skip[l].reshape(groups, 1, SSM_GROUP), (1, 1, CHUNK_T))
    yp, ys, lp_re, lp_im, ls_re, ls_im = _ssm(
        up, us,
        state_ssm_re[l].astype(F32).transpose(1, 0, 2), state_ssm_im[l].astype(F32).transpose(1, 0, 2),
        lam_re[l], lam_im[l], log_dt[l], bt(b_re[l]), bt(b_im[l]), ct(c_re[l]), ct(c_im[l]), d_t,
        streams_p=bp)

    glu = functools.partial(_glu, w=bf(w_glu[l]), **TILES["glu"])
    merge = functools.partial(_merge, g_attn=out_norm_attn[l], g_ssm=out_norm_ssm[l], w_out=bf(w_out[l]),
                              **TILES["merge"])
    xp = merge(o_p.reshape(bp * sp, da), glu(yp), xp)
    xs = merge(o_s.reshape(bs * ts, da), glu(ys), xs)

    ffn_b = functools.partial(_ffn, g=norm_ffn2[l], w1=bf(w1_b[l]), w3=bf(w3_b[l]), w2=bf(w2_b[l]), **TILES["ffn"])
    xp = ffn_b(xp)
    xs = ffn_b(xs)

    return (xp.reshape(bp, sp, d), xs.reshape(bs, ts, d),
            kp.reshape(1, bp, sp, nh, HEAD_DIM), vp.reshape(1, bp, sp, nh, HEAD_DIM), lfp.reshape(1, bp, sp, nh),
            lp_re[None], lp_im[None],
            ks.reshape(1, bs, ts, nh, HEAD_DIM), vs.reshape(1, bs, ts, nh, HEAD_DIM), lfs.reshape(1, bs, ts, nh),
            ls_re.transpose(1, 0, 2)[None], ls_im.transpose(1, 0, 2)[None])
```

```python
import functools
import math

import jax
import jax.numpy as jnp
import numpy as np
from jax import lax
from jax.experimental import pallas as pl
from jax.experimental.pallas import tpu as pltpu

F32 = jnp.float32
BF16 = jnp.bfloat16

EPS = 1e-6
NEG_INF = -1e30
LOG2E = math.log2(math.e)
HEAD_DIM = 128
SSM_GROUP = 16
SSM_STATE = 64
LANES = 128
MXU_COLS = 256
CHUNK_T = LANES // SSM_GROUP
GROUPS_PER_STEP = 8
GROUPS_PER_GRID_STEP = 2
VMEM_LIMIT = 56 * 1024 * 1024


def _cparams(sem):
    return pltpu.CompilerParams(dimension_semantics=sem, vmem_limit_bytes=VMEM_LIMIT)


def _rms(x, g):
    ms = jnp.mean(x * x, axis=-1, keepdims=True)
    return x * lax.rsqrt(ms + EPS) * g


def _dot(a, b):
    return jnp.dot(a, b, preferred_element_type=F32)


def _dot_nt(a, b):
    return lax.dot_general(a, b, (((1,), (1,)), ((), ())), preferred_element_type=F32)


def _ffn_up_kernel(x_ref, g_ref, w1_ref, w3_ref, *rest, n_cast):
    cast_in, act_ref, cast_out, h_ref = rest[:n_cast], rest[n_cast], rest[n_cast + 1:-1], rest[-1]

    @pl.when(pl.program_id(1) == 0)
    def _():
        h_ref[...] = _rms(x_ref[...], g_ref[...]).astype(BF16)

    h = h_ref[...]
    a = _dot(h, w1_ref[...])
    b = _dot(h, w3_ref[...])
    act_ref[...] = (0.5 * a * jax.nn.sigmoid(a) * b).astype(BF16)

    for src, dst in zip(cast_in, cast_out):
        dst[...] = src[...].astype(BF16)


def _ffn_down_kernel(act_ref, w2_ref, x_ref, o_ref):
    o_ref[...] = x_ref[...] + _dot(act_ref[...], w2_ref[...])


def _ffn_up(x, g, w1, w3, *, tm, tf, cast=(), cast_by="tiles"):
    t, d = x.shape
    f = w1.shape[1]
    tm, tf = min(tm, t), min(tf, f)
    ni, nj = t // tm, f // tf
    cast_specs = []
    for m in cast:
        r, c = m.shape
        if cast_by == "tiles":
            assert r % ni == 0 and c % nj == 0
            cast_specs.append(pl.BlockSpec((r // ni, c // nj), lambda i, j: (i, j)))
        else:
            assert r % (ni * nj) == 0
            cast_specs.append(pl.BlockSpec((r // (ni * nj), c), lambda i, j: (i * nj + j, 0)))
    outs = pl.pallas_call(
        functools.partial(_ffn_up_kernel, n_cast=len(cast)),
        grid=(ni, nj),
        in_specs=[
            pl.BlockSpec((tm, d), lambda i, j: (i, 0), pipeline_mode=pl.Buffered(1)),
            pl.BlockSpec((1, d), lambda i, j: (0, 0)),
            pl.BlockSpec((d, tf), lambda i, j: (0, j)),
            pl.BlockSpec((d, tf), lambda i, j: (0, j)),
        ] + cast_specs,
        out_specs=[pl.BlockSpec((tm, tf), lambda i, j: (i, j))] + cast_specs,
        out_shape=[jax.ShapeDtypeStruct((t, f), BF16)] + [jax.ShapeDtypeStruct(m.shape, BF16) for m in cast],
        scratch_shapes=[pltpu.VMEM((tm, d), BF16)],
        compiler_params=_cparams(("parallel", "arbitrary")),
        name="ffn_up",
    )(x, g.reshape(1, d), w1, w3, *cast)
    return outs[0], outs[1:]


def _ffn_down(act, w2, x, *, tm, tn):
    t, d = x.shape
    f = act.shape[1]
    tm, tn = min(tm, t), min(tn, d)
    return pl.pallas_call(
        _ffn_down_kernel,
        grid=(t // tm, d // tn),
        in_specs=[
            pl.BlockSpec((tm, f), lambda i, n: (i, 0)),
            pl.BlockSpec((f, tn), lambda i, n: (0, n)),
            pl.BlockSpec((tm, tn), lambda i, n: (i, n)),
        ],
        out_specs=pl.BlockSpec((tm, tn), lambda i, n: (i, n)),
        out_shape=jax.ShapeDtypeStruct((t, d), F32),
        compiler_params=_cparams(("parallel", "arbitrary")),
        name="ffn_down",
    )(act, w2, x)


def _inproj_kernel(x_ref, g_ref, w_ref, wf_ref, bf_ref, qn_ref, kn_ref,
                   q_ref, k_ref, kb_ref, v_ref, vb_ref, u_ref, lf_ref, h_ref, *, tiles):
    n = pl.program_id(1)

    @pl.when(n == 0)
    def _():
        h = _rms(x_ref[...], g_ref[...]).astype(BF16)
        h_ref[...] = h
        zf = _dot_nt(h, wf_ref[...]) + bf_ref[...]
        lf_ref[...] = jnp.minimum(zf, 0.0) - jnp.log1p(jnp.exp(-jnp.abs(zf)))

    sec = n // tiles
    tn = w_ref.shape[0]
    cw = min(MXU_COLS, tn)

    def chunks(weights):
        for c in range(tn // cw):
            cols = slice(c * cw, (c + 1) * cw)
            yield cols, _dot_nt(h_ref[...], weights[cols, :])

    def head_norm(z, gain):
        outs = [_rms(z[:, hh * HEAD_DIM:(hh + 1) * HEAD_DIM], gain) for hh in range(cw // HEAD_DIM)]
        return outs[0] if len(outs) == 1 else jnp.concatenate(outs, axis=-1)

    @pl.when(sec == 0)
    def _():
        for cols, z in chunks(w_ref):
            q_ref[:, cols] = head_norm(z, qn_ref[...]).astype(BF16)

    @pl.when(sec == 1)
    def _():
        for cols, z in chunks(w_ref):
            kk = head_norm(z, kn_ref[...])
            k_ref[:, cols] = kk
            kb_ref[:, cols] = kk.astype(BF16)

    @pl.when(sec == 2)
    def _():
        for cols, z in chunks(w_ref):
            v_ref[:, cols] = z
            vb_ref[:, cols] = z.astype(BF16)

    @pl.when(sec >= 3)
    def _():
        for cols, z in chunks(w_ref):
            u_ref[:, cols] = z


def _inproj(x, g, w_main, w_f, b_f, q_norm, k_norm, *, da, tm, tn):
    t, d = x.shape
    du = w_main.shape[0] - 3 * da
    nh = w_f.shape[0]
    tm, tn = min(tm, t), min(tn, da, du)
    tiles, tiles_u = da // tn, du // tn

    def sec_map(s, count):
        return lambda i, n: (i, jnp.clip(n - s * tiles, 0, count - 1))

    row = lambda i, n: (0, 0)
    attn_out = lambda dt: jax.ShapeDtypeStruct((t, da), dt)
    return pl.pallas_call(
        functools.partial(_inproj_kernel, tiles=tiles),
        grid=(t // tm, 3 * tiles + tiles_u),
        in_specs=[
            pl.BlockSpec((tm, d), lambda i, n: (i, 0), pipeline_mode=pl.Buffered(1)),
            pl.BlockSpec((1, d), row),
            pl.BlockSpec((tn, d), lambda i, n: (n, 0)),
            pl.BlockSpec((nh, d), row),
            pl.BlockSpec((1, nh), row),
            pl.BlockSpec((1, HEAD_DIM), row),
            pl.BlockSpec((1, HEAD_DIM), row),
        ],
        out_specs=[
            pl.BlockSpec((tm, tn), sec_map(0, tiles)),
            pl.BlockSpec((tm, tn), sec_map(1, tiles)),
            pl.BlockSpec((tm, tn), sec_map(1, tiles)),
            pl.BlockSpec((tm, tn), sec_map(2, tiles)),
            pl.BlockSpec((tm, tn), sec_map(2, tiles)),
            pl.BlockSpec((tm, tn), sec_map(3, tiles_u)),
            pl.BlockSpec((tm, nh), lambda i, n: (i, 0)),
        ],
        out_shape=[attn_out(BF16), attn_out(F32), attn_out(BF16), attn_out(F32), attn_out(BF16),
                   jax.ShapeDtypeStruct((t, du), F32), jax.ShapeDtypeStruct((t, nh), F32)],
        scratch_shapes=[pltpu.VMEM((tm, d), BF16)],
        compiler_params=_cparams(("parallel", "arbitrary")),
        name="inproj",
    )(x, g.reshape(1, d), w_main, w_f, b_f.reshape(1, nh),
      q_norm.reshape(1, HEAD_DIM), k_norm.reshape(1, HEAD_DIM))


def _split3(x):
    hi = x.astype(BF16)
    r1 = x - hi.astype(F32)
    mid = r1.astype(BF16)
    lo = (r1 - mid.astype(F32)).astype(BF16)
    return hi, mid, lo


def _cumsum_kernel(x_ref, o_ref, carry_ref):
    @pl.when(pl.program_id(0) == 0)
    def _():
        carry_ref[...] = jnp.zeros_like(carry_ref)

    x = x_ref[...]
    bs = x.shape[1]
    row = lax.broadcasted_iota(jnp.int32, (bs, bs), 0)
    col = lax.broadcasted_iota(jnp.int32, (bs, bs), 1)
    tri = jnp.where(row <= col, 1.0, 0.0).astype(BF16)
    hi, mid, lo = _split3(x)
    c = _dot(hi, tri) + _dot(mid, tri) + _dot(lo, tri) + carry_ref[...]
    o_ref[...] = c
    carry_ref[...] = c[:, bs - 1:bs]


def _cumsum_rows(x, *, bs):
    r, s = x.shape
    bs = min(bs, s)
    return pl.pallas_call(
        _cumsum_kernel,
        grid=(s // bs,),
        in_specs=[pl.BlockSpec((r, bs), lambda j: (0, j))],
        out_specs=pl.BlockSpec((r, bs), lambda j: (0, j)),
        out_shape=jax.ShapeDtypeStruct((r, s), F32),
        scratch_shapes=[pltpu.VMEM((r, 1), F32)],
        compiler_params=_cparams(("arbitrary",)),
        name="cumsum",
    )(x)


def _lane_tile(x, width):
    return x if width == LANES else jnp.tile(x, (1, width // LANES))


def _attn_prompt_kernel(qi_tab, ki_tab, q_ref, k_ref, v_ref, ccol_ref, crow_ref, o_ref,
                        m_ref, l_ref, acc_ref, cq_ref, *, heads, scale):
    t = pl.program_id(2)
    qi, ki = qi_tab[t], ki_tab[t]
    tq, tk = q_ref.shape[1], k_ref.shape[1]

    @pl.when(ki == 0)
    def _():
        m_ref[...] = jnp.full_like(m_ref, NEG_INF)
        l_ref[...] = jnp.zeros_like(l_ref)
        acc_ref[...] = jnp.zeros_like(acc_ref)
        for hh in range(heads):
            cq_ref[hh] = jnp.broadcast_to(ccol_ref[0, hh] * LOG2E, (tq, LANES))

    def step(masked):
        for hh in range(heads):
            sl = slice(hh * HEAD_DIM, (hh + 1) * HEAD_DIM)
            s = _dot_nt(q_ref[0, :, sl], k_ref[0, :, sl]) * (scale * LOG2E)
            s = s + _lane_tile(cq_ref[hh], tk) - crow_ref[0, hh] * LOG2E
            if masked:
                row = lax.broadcasted_iota(jnp.int32, (tq, tk), 0)
                col = lax.broadcasted_iota(jnp.int32, (tq, tk), 1)
                s = jnp.where(col <= row, s, NEG_INF)
            m_prev = m_ref[hh]
            m_new = jnp.maximum(m_prev, jnp.max(s, axis=-1, keepdims=True))
            alpha = jnp.exp2(m_prev - m_new)
            p = jnp.exp2(s - _lane_tile(m_new, tk))
            l_ref[hh] = alpha * l_ref[hh] + jnp.sum(p, axis=-1, keepdims=True)
            acc_ref[hh] = alpha * acc_ref[hh] + _dot(p.astype(BF16), v_ref[0, :, sl])
            m_ref[hh] = m_new

    @pl.when(ki < qi)
    def _():
        step(False)

    @pl.when(ki == qi)
    def _():
        step(True)
        for hh in range(heads):
            o_ref[0, :, hh * HEAD_DIM:(hh + 1) * HEAD_DIM] = acc_ref[hh] / l_ref[hh]


def _attn_prompt(q, kb, vb, c_t, *, tq, heads):
    b, s, da = q.shape
    nh = da // HEAD_DIM
    tq = min(tq, s)
    heads = min(heads, nh)
    nq = s // tq
    pairs = [(i, j) for i in range(nq) for j in range(i + 1)]
    qi_tab = jnp.asarray([p[0] for p in pairs], jnp.int32)
    ki_tab = jnp.asarray([p[1] for p in pairs], jnp.int32)
    w = heads * HEAD_DIM
    grid_spec = pltpu.PrefetchScalarGridSpec(
        num_scalar_prefetch=2,
        grid=(b, nh // heads, len(pairs)),
        in_specs=[
            pl.BlockSpec((1, tq, w), lambda bi, hg, t, qt, kt: (bi, qt[t], hg)),
            pl.BlockSpec((1, tq, w), lambda bi, hg, t, qt, kt: (bi, kt[t], hg)),
            pl.BlockSpec((1, tq, w), lambda bi, hg, t, qt, kt: (bi, kt[t], hg)),
            pl.BlockSpec((1, heads, tq, 1), lambda bi, hg, t, qt, kt: (bi, hg, qt[t], 0)),
            pl.BlockSpec((1, heads, 1, tq), lambda bi, hg, t, qt, kt: (bi, hg, 0, kt[t])),
        ],
        out_specs=pl.BlockSpec((1, tq, w), lambda bi, hg, t, qt, kt: (bi, qt[t], hg)),
        scratch_shapes=[
            pltpu.VMEM((heads, tq, LANES), F32),
            pltpu.VMEM((heads, tq, LANES), F32),
            pltpu.VMEM((heads, tq, HEAD_DIM), F32),
            pltpu.VMEM((heads, tq, LANES), F32),
        ],
    )
    return pl.pallas_call(
        functools.partial(_attn_prompt_kernel, heads=heads, scale=HEAD_DIM ** -0.5),
        grid_spec=grid_spec,
        out_shape=jax.ShapeDtypeStruct((b, s, da), F32),
        compiler_params=_cparams(("parallel", "parallel", "arbitrary")),
        name="attn_prompt",
    )(qi_tab, ki_tab, q, kb, vb, c_t.reshape(b, nh, s, 1), c_t.reshape(b, nh, 1, s))


def _attn_sample_kernel(q_ref, ck_ref, cv_ref, kn_ref, vn_ref, crow_ref, ccol_ref, o_ref, kq_ref, vq_ref, *, scale):
    t = q_ref.shape[1]
    nh = q_ref.shape[2] // HEAD_DIM
    past = ck_ref.shape[1] // nh
    row = lax.broadcasted_iota(jnp.int32, (t, t), 0)
    col = lax.broadcasted_iota(jnp.int32, (t, t), 1)

    s1 = kq_ref.shape[0]
    s2 = nh // s1
    for r in range(s1):
        kq_ref[r] = ck_ref[0, pl.ds(r, past * s2, stride=s1), :]
        vq_ref[r] = cv_ref[0, pl.ds(r, past * s2, stride=s1), :]

    for hh in range(nh):
        sl = slice(hh * HEAD_DIM, (hh + 1) * HEAD_DIM)
        r, a = hh % s1, hh // s1
        cached = pl.ds(a, past, stride=s2) if s2 > 1 else slice(None)
        qh = q_ref[0, :, sl]
        cq = ccol_ref[0, :, hh:hh + 1]
        ck = crow_ref[0, hh:hh + 1, :]
        sc = _dot_nt(qh, kq_ref[r, cached, :].astype(BF16)) * scale + cq - ck[:, :past]
        sn = _dot_nt(qh, kn_ref[0, :, sl]) * scale + cq - ck[:, past:]
        sn = jnp.where(col <= row, sn, NEG_INF)
        m = jnp.maximum(jnp.max(sc, axis=-1, keepdims=True), jnp.max(sn, axis=-1, keepdims=True))
        pc = jnp.exp(sc - m)
        pn = jnp.exp(sn - m)
        l = jnp.sum(pc, axis=-1, keepdims=True) + jnp.sum(pn, axis=-1, keepdims=True)
        o = _dot(pc.astype(BF16), vq_ref[r, cached, :].astype(BF16)) + _dot(pn.astype(BF16), vn_ref[0, :, sl])
        o_ref[0, :, sl] = o / l


def _attn_sample(q, cache_k, cache_v, kb, vb, c_t, c_col):
    b, t, da = q.shape
    nh = da // HEAD_DIM
    rows = cache_k.shape[1]
    first_stride = math.gcd(nh, 4)
    blk = lambda *shape: pl.BlockSpec((1,) + shape, lambda i: (i, 0, 0))
    return pl.pallas_call(
        functools.partial(_attn_sample_kernel, scale=HEAD_DIM ** -0.5),
        grid=(b,),
        in_specs=[blk(t, da), blk(rows, HEAD_DIM), blk(rows, HEAD_DIM), blk(t, da), blk(t, da),
                  blk(nh, rows // nh + t), blk(t, nh)],
        out_specs=blk(t, da),
        out_shape=jax.ShapeDtypeStruct((b, t, da), F32),
        scratch_shapes=[pltpu.VMEM((first_stride, rows // first_stride, HEAD_DIM), F32)] * 2,
        compiler_params=_cparams(("parallel",)),
        name="attn_sample",
    )(q, cache_k, cache_v, kb, vb, c_t, c_col)


def _cexp(mag_log, ang):
    mag = jnp.exp(mag_log)
    return mag * jnp.cos(ang), mag * jnp.sin(ang)


def _cmul(ar, ai, br, bi):
    return ar * br - ai * bi, ar * bi + ai * br


def _cpow_table(zr, zi, k):
    acc_r = jnp.broadcast_to(zr, k.shape)
    acc_i = jnp.broadcast_to(zi, k.shape)
    bit = 1
    while bit < CHUNK_T:
        take = (k & bit) != 0
        acc_r, acc_i = _cmul(acc_r, acc_i, jnp.where(take, zr, 1.0), jnp.where(take, zi, 0.0))
        zr, zi = _cmul(zr, zi, zr, zi)
        bit *= 2
    return acc_r, acc_i


def _gelu_tanh(x):
    return 0.5 * x * (1.0 + jnp.tanh(math.sqrt(2.0 / math.pi) * (x + 0.044715 * (x * x * x))))


def _split2(x):
    hi = x.astype(BF16)
    lo = (x - hi.astype(F32)).astype(BF16)
    return hi, lo


def _lane_block(rows):
    return lax.broadcasted_iota(jnp.int32, (rows, LANES), 1) // SSM_GROUP


def _block_transpose(xs):
    blk = _lane_block(xs[0].shape[0])
    s = CHUNK_T // 2
    while s:
        upper = (blk & s) != 0
        nxt = list(xs)
        for a in range(CHUNK_T):
            if a & s:
                continue
            lo, hi = xs[a], xs[a + s]
            nxt[a] = jnp.where(upper, pltpu.roll(hi, s * SSM_GROUP, axis=1), lo)
            nxt[a + s] = jnp.where(upper, hi, pltpu.roll(lo, (CHUNK_T - s) * SSM_GROUP, axis=1))
        xs = nxt
        s //= 2
    return xs


def _to_chunk_layout(src_ref, dst_ref):
    rows = dst_ref.shape[1]
    toks = [src_ref[pl.ds(t, rows, stride=CHUNK_T), :] for t in range(CHUNK_T)]
    for i, x in enumerate(_block_transpose(toks)):
        dst_ref[i] = x


def _from_chunk_layout(src_ref, dst_ref):
    rows = src_ref.shape[1]
    for t, x in enumerate(_block_transpose([src_ref[i] for i in range(CHUNK_T)])):
        dst_ref[pl.ds(t, rows, stride=CHUNK_T), :] = x


def _ssm_kernel(up_ref, us_ref, x0r_ref, x0i_ref, lrow_re_ref, lrow_im_ref, lcol_re_ref, lcol_im_ref,
                ldt_ref, l8_re_ref, l8_im_ref, ldt8_ref, bt_re_ref, bt_im_ref, ct_re_ref, ct_im_ref, d_ref,
                yp_ref, ys_ref, lastp_re_ref, lastp_im_ref, lasts_re_ref, lasts_im_ref,
                cp_ref, cs_ref, wr_ref, wi_ref, xr_ref, xi_ref, wsr_ref, wsi_ref, xsr_ref, xsi_ref,
                vre_ref, vim_ref, *, streams_p):
    j = pl.program_id(1)
    gb, gs = GROUPS_PER_STEP, GROUPS_PER_GRID_STEP
    rp = cp_ref.shape[1]
    p = SSM_STATE

    @pl.when(j == 0)
    def _():
        _to_chunk_layout(up_ref, cp_ref)
        _to_chunk_layout(us_ref, cs_ref)

    for jj in range(gs):
        gl = j * gs + jj
        dt = jnp.exp(ldt_ref[jj])

        a_row = lrow_re_ref[jj] * dt
        w_row = lrow_im_ref[jj] * dt
        s_idx = lax.broadcasted_iota(jnp.int32, (LANES, p), 0) // SSM_GROUP
        en_re, en_im = _cpow_table(*_cexp(-a_row, -w_row), s_idx)
        l8_re, l8_im = _cexp(CHUNK_T * a_row, CHUNK_T * w_row)
        lb_re, lb_im = _cexp(a_row, w_row)
        lr, li = lrow_re_ref[jj], lrow_im_ref[jj]
        inv = 1.0 / (lr * lr + li * li)
        cf_re, cf_im = _cmul(lb_re - 1.0, lb_im, lr * inv, -li * inv)
        bb_re, bb_im = _cmul(cf_re, cf_im, bt_re_ref[jj], bt_im_ref[jj])
        f_re, f_im = _cmul(en_re, en_im, bb_re, bb_im)
        e7_re, e7_im = _cmul(en_re, en_im, l8_re, l8_im)
        w_re, w_im = _cmul(e7_re, e7_im, bb_re, bb_im)
        w_re, w_im = w_re.astype(BF16), w_im.astype(BF16)

        a_col = lcol_re_ref[jj] * dt
        w_col = lcol_im_ref[jj] * dt
        t_idx = lax.broadcasted_iota(jnp.int32, (p, LANES), 1) // SSM_GROUP
        et_re, et_im = _cpow_table(*_cexp(a_col, w_col), t_idx)
        g_re, g_im = _cmul(et_re, et_im, ct_re_ref[jj], ct_im_ref[jj])
        v_re, v_im = g_re.astype(BF16), (-g_im).astype(BF16)
        vre_ref[gl] = v_re
        vim_ref[gl] = v_im

        fs = jnp.concatenate([f_re, -f_im], axis=1)
        hs = jnp.concatenate([g_re, g_im], axis=0)
        fh, fl = _split2(fs)
        hh, hl = _split2(hs)
        tmat = _dot(fh, hh) + _dot(fh, hl) + _dot(fl, hh)
        srow = lax.broadcasted_iota(jnp.int32, (LANES, LANES), 0) // SSM_GROUP
        tcol = lax.broadcasted_iota(jnp.int32, (LANES, LANES), 1) // SSM_GROUP
        tmat = jnp.where(srow <= tcol, tmat, 0.0).astype(BF16)

        d_row = d_ref[jj]

        u = cp_ref[gl]
        ub = u.astype(BF16)
        cp_ref[gl] = _dot(ub, tmat) + d_row * u
        base = pl.multiple_of(gl * rp, 8)
        wr_ref[pl.ds(base, rp), :] = _dot(ub, w_re)
        wi_ref[pl.ds(base, rp), :] = _dot(ub, w_im)

        us = cs_ref[gl]
        usb = us.astype(BF16)
        wsr_ref[...] = _dot(usb, w_re)
        wsi_ref[...] = _dot(usb, w_im)
        nb = x0r_ref.shape[1]
        ncs = us.shape[0] // nb
        sr, si = x0r_ref[jj], x0i_ref[jj]
        for c in range(ncs):
            rows = pl.ds(c, nb, stride=ncs)
            xsr_ref[rows, :] = sr
            xsi_ref[rows, :] = si
            nr, ni = _cmul(l8_re, l8_im, sr, si)
            sr = nr + wsr_ref[rows, :]
            si = ni + wsi_ref[rows, :]
        lasts_re_ref[jj] = sr
        lasts_im_ref[jj] = si
        xs_re = xsr_ref[...].astype(BF16)
        xs_im = xsi_ref[...].astype(BF16)
        cs_ref[gl] = _gelu_tanh(_dot(usb, tmat) + _dot(xs_re, v_re) + _dot(xs_im, v_im) + d_row * us)

    @pl.when(j == gb // gs - 1)
    def _():
        dt8 = jnp.exp(ldt8_ref[0])
        a8_re, a8_im = _cexp(CHUNK_T * l8_re_ref[0] * dt8, CHUNK_T * l8_im_ref[0] * dt8)
        nc = rp // streams_p

        def body(c, carry):
            new = []
            for b in range(streams_p):
                sr, si = carry[2 * b], carry[2 * b + 1]
                r = b * nc + c
                xr_ref[pl.ds(r, gb, stride=rp), :] = sr
                xi_ref[pl.ds(r, gb, stride=rp), :] = si
                nr, ni = _cmul(a8_re, a8_im, sr, si)
                new.append(nr + wr_ref[pl.ds(r, gb, stride=rp), :])
                new.append(ni + wi_ref[pl.ds(r, gb, stride=rp), :])
            return tuple(new)

        zero = jnp.zeros((gb, p), F32)
        fin = lax.fori_loop(0, nc, body, (zero,) * (2 * streams_p), unroll=4)
        for b in range(streams_p):
            lastp_re_ref[b] = fin[2 * b]
            lastp_im_ref[b] = fin[2 * b + 1]

        for g in range(gb):
            xr = xr_ref[g * rp:(g + 1) * rp, :].astype(BF16)
            xi = xi_ref[g * rp:(g + 1) * rp, :].astype(BF16)
            cp_ref[g] = _gelu_tanh(cp_ref[g] + _dot(xr, vre_ref[g]) + _dot(xi, vim_ref[g]))

        _from_chunk_layout(cp_ref, yp_ref)
        _from_chunk_layout(cs_ref, ys_ref)


def _ssm(up, us, x0_re, x0_im, lam_re, lam_im, log_dt, bt_re, bt_im, ct_re, ct_im, d_t, *, streams_p):
    tp, dssm = up.shape
    ts = us.shape[0]
    g = dssm // SSM_GROUP
    rp, rs = tp // CHUNK_T, ts // CHUNK_T
    nb = x0_re.shape[1]
    p = SSM_STATE
    gb, gs = GROUPS_PER_STEP, GROUPS_PER_GRID_STEP
    ngb = g // gb
    per_group = lambda *shape: pl.BlockSpec((gs,) + shape, lambda i, j: (i * (gb // gs) + j, 0, 0))
    per_block = lambda *shape: pl.BlockSpec((1,) + shape, lambda i, j: (i, 0, 0))
    tokens = lambda rows, **kw: pl.BlockSpec((rows, LANES), lambda i, j: (0, i), **kw)
    return pl.pallas_call(
        functools.partial(_ssm_kernel, streams_p=streams_p),
        grid=(ngb, gb // gs),
        in_specs=[
            tokens(tp, pipeline_mode=pl.Buffered(1)), tokens(ts, pipeline_mode=pl.Buffered(1)),
            per_group(nb, p), per_group(nb, p),
            per_group(1, p), per_group(1, p), per_group(p, 1), per_group(p, 1), per_group(1, 1),
            per_block(gb, p), per_block(gb, p), per_block(gb, 1),
            per_group(LANES, p), per_group(LANES, p), per_group(p, LANES), per_group(p, LANES),
            per_group(1, LANES),
        ],
        out_specs=[
            tokens(tp), tokens(ts),
            pl.BlockSpec((streams_p, gb, p), lambda i, j: (0, i, 0)),
            pl.BlockSpec((streams_p, gb, p), lambda i, j: (0, i, 0)),
            per_group(nb, p), per_group(nb, p),
        ],
        out_shape=[
            jax.ShapeDtypeStruct((tp, dssm), F32),
            jax.ShapeDtypeStruct((ts, dssm), F32),
            jax.ShapeDtypeStruct((streams_p, g, p), F32),
            jax.ShapeDtypeStruct((streams_p, g, p), F32),
            jax.ShapeDtypeStruct((g, nb, p), F32),
            jax.ShapeDtypeStruct((g, nb, p), F32),
        ],
        scratch_shapes=[
            pltpu.VMEM((gb, rp, LANES), F32), pltpu.VMEM((gb, rs, LANES), F32),
            pltpu.VMEM((gb * rp, p), F32), pltpu.VMEM((gb * rp, p), F32),
            pltpu.VMEM((gb * rp, p), F32), pltpu.VMEM((gb * rp, p), F32),
            pltpu.VMEM((rs, p), F32), pltpu.VMEM((rs, p), F32),
            pltpu.VMEM((rs, p), F32), pltpu.VMEM((rs, p), F32),
            pltpu.VMEM((gb, p, LANES), BF16), pltpu.VMEM((gb, p, LANES), BF16),
        ],
        compiler_params=_cparams(("parallel", "arbitrary")),
        name="ssm",
    )(up, us, x0_re, x0_im,
      lam_re.reshape(g, 1, p), lam_im.reshape(g, 1, p), lam_re.reshape(g, p, 1), lam_im.reshape(g, p, 1),
      log_dt.reshape(g, 1, 1),
      lam_re.reshape(ngb, gb, p), lam_im.reshape(ngb, gb, p), log_dt.reshape(ngb, gb, 1),
      bt_re, bt_im, ct_re, ct_im, d_t)


def _glu_kernel(y_ref, w_ref, o_ref):
    y = y_ref[...]
    o_ref[...] = y * jax.nn.sigmoid(_dot(y.astype(BF16), w_ref[...]))


def _glu(y, w, *, tm):
    t, d = y.shape
    tm = min(tm, t)
    return pl.pallas_call(
        _glu_kernel,
        grid=(t // tm,),
        in_specs=[pl.BlockSpec((tm, d), lambda i: (i, 0)), pl.BlockSpec((d, d), lambda i: (0, 0))],
        out_specs=pl.BlockSpec((tm, d), lambda i: (i, 0)),
        out_shape=jax.ShapeDtypeStruct((t, d), F32),
        compiler_params=_cparams(("parallel",)),
        name="glu",
    )(y, w)


def _merge_kernel(o_ref, y_ref, x_ref, ga_ref, gs_ref, w_ref, out_ref, a_ref):
    da = o_ref.shape[1]

    @pl.when(pl.program_id(1) == 0)
    def _():
        a_ref[:, :da] = _rms(o_ref[...], ga_ref[...]).astype(BF16)
        a_ref[:, da:] = _rms(y_ref[...], gs_ref[...]).astype(BF16)

    out_ref[...] = x_ref[...] + _dot(a_ref[...], w_ref[...])


def _merge(o, y, x, g_attn, g_ssm, w_out, *, tm, tn):
    t, da = o.shape
    ds = y.shape[1]
    d = x.shape[1]
    tm, tn = min(tm, t), min(tn, d)
    return pl.pallas_call(
        _merge_kernel,
        grid=(t // tm, d // tn),
        in_specs=[
            pl.BlockSpec((tm, da), lambda i, n: (i, 0), pipeline_mode=pl.Buffered(1)),
            pl.BlockSpec((tm, ds), lambda i, n: (i, 0), pipeline_mode=pl.Buffered(1)),
            pl.BlockSpec((tm, tn), lambda i, n: (i, n)),
            pl.BlockSpec((1, da), lambda i, n: (0, 0)),
            pl.BlockSpec((1, ds), lambda i, n: (0, 0)),
            pl.BlockSpec((da + ds, tn), lambda i, n: (0, n)),
        ],
        out_specs=pl.BlockSpec((tm, tn), lambda i, n: (i, n)),
        out_shape=jax.ShapeDtypeStruct((t, d), F32),
        scratch_shapes=[pltpu.VMEM((tm, da + ds), BF16)],
        compiler_params=_cparams(("parallel", "arbitrary")),
        name="merge",
    )(o, y, x, g_attn.reshape(1, da), g_ssm.reshape(1, ds), w_out)


TILES = dict(
    ffn_up=dict(tm=1024, tf=256),
    ffn_down=dict(tm=512, tn=512),
    inproj=dict(tm=1024, tn=512),
    attn_prompt=dict(tq=512, heads=8),
    cumsum=dict(bs=512),
    glu=dict(tm=512),
    merge=dict(tm=1024, tn=512),
)


def kernel(x_prompt, x_sample, cache_k, cache_v, cache_logf, state_ssm_re, state_ssm_im, norm_ffn1, w1_a, w3_a, w2_a, norm_mix, w_in, b_f, q_norm, k_norm, lam_re, lam_im, log_dt, b_re, b_im, c_re, c_im, d_skip, w_glu, out_norm_attn, out_norm_ssm, w_out, norm_ffn2, w1_b, w3_b, w2_b):
    depth = norm_ffn1.shape[0]
    assert depth == 1
    l = 0
    bp, sp, d = x_prompt.shape
    bs, ts, _ = x_sample.shape
    past = cache_k.shape[2]
    nh = b_f.shape[1]
    da = nh * HEAD_DIM
    groups = lam_re.shape[1]
    dssm = groups * SSM_GROUP
    p = SSM_STATE

    bf = lambda w: w.astype(BF16)
    w_in_t = jnp.swapaxes(w_in[l], 0, 1)
    w_main = bf(jnp.concatenate([w_in_t[:3 * da], w_in_t[3 * da + nh:]], axis=0))
    w_f = bf(w_in_t[3 * da:3 * da + nh])

    xp = x_prompt.reshape(bp * sp, d)
    xs = x_sample.reshape(bs * ts, d)

    def ffn(xs, xp, g, w1, w3, w2_f32, later=()):
        act_s, (w2,) = _ffn_up(xs, g, w1, w3, cast=(w2_f32,), cast_by="slabs", **TILES["ffn_up"])
        xs = _ffn_down(act_s, w2, xs, **TILES["ffn_down"])
        act_p, later = _ffn_up(xp, g, w1, w3, cast=later, cast_by="tiles", **TILES["ffn_up"])
        xp = _ffn_down(act_p, w2, xp, **TILES["ffn_down"])
        return xs, xp, later

    xs, xp, (w1_b16, w3_b16) = ffn(xs, xp, norm_ffn1[l], bf(w1_a[l]), bf(w3_a[l]), w2_a[l],
                                   later=(w1_b[l], w3_b[l]))

    proj = functools.partial(_inproj, g=norm_mix[l], w_main=w_main, w_f=w_f, b_f=b_f[l], da=da,
                             q_norm=q_norm[l], k_norm=k_norm[l], **TILES["inproj"])
    qp, kp, kbp, vp, vbp, up, lfp = proj(xp)
    qs, ks, kbs, vs, vbs, us, lfs = proj(xs)

    lfp_t = lfp.reshape(bp, sp, nh).transpose(0, 2, 1).reshape(bp * nh, sp)
    cp_t = _cumsum_rows(lfp_t, **TILES["cumsum"]).reshape(bp, nh, sp)
    o_p = _attn_prompt(qp.reshape(bp, sp, da), kbp.reshape(bp, sp, da), vbp.reshape(bp, sp, da), cp_t,
                       **TILES["attn_prompt"])

    lfs_all = jnp.concatenate([cache_logf[l].astype(F32), lfs.reshape(bs, ts, nh)], axis=1)
    lfs_t = lfs_all.transpose(0, 2, 1).reshape(bs * nh, past + ts)
    cs_t = _cumsum_rows(lfs_t, bs=past + ts).reshape(bs, nh, past + ts)
    cs_col = cs_t[:, :, past:].transpose(0, 2, 1)
    o_s = _attn_sample(qs.reshape(bs, ts, da),
                       cache_k[l].reshape(bs, past * nh, HEAD_DIM), cache_v[l].reshape(bs, past * nh, HEAD_DIM),
                       kbs.reshape(bs, ts, da), vbs.reshape(bs, ts, da), cs_t, cs_col)

    bt = lambda b: jnp.tile(b.transpose(0, 2, 1), (1, CHUNK_T, 1))
    ct = lambda c: jnp.tile(c.transpose(0, 2, 1), (1, 1, CHUNK_T))
    d_t = jnp.tile(d_skip[l].reshape(groups, 1, SSM_GROUP), (1, 1, CHUNK_T))
    yp, ys, lp_re, lp_im, ls_re, ls_im = _ssm(
        up, us,
        state_ssm_re[l].astype(F32).transpose(1, 0, 2), state_ssm_im[l].astype(F32).transpose(1, 0, 2),
        lam_re[l], lam_im[l], log_dt[l], bt(b_re[l]), bt(b_im[l]), ct(c_re[l]), ct(c_im[l]), d_t,
        streams_p=bp)

    glu = functools.partial(_glu, w=bf(w_glu[l]), **TILES["glu"])
    merge = functools.partial(_merge, g_attn=out_norm_attn[l], g_ssm=out_norm_ssm[l], w_out=bf(w_out[l]),
                              **TILES["merge"])
    xp = merge(o_p.reshape(bp * sp, da), glu(yp), xp)
    xs = merge(o_s.reshape(bs * ts, da), glu(ys), xs)

    xs, xp, _ = ffn(xs, xp, norm_ffn2[l], w1_b16, w3_b16, w2_b[l])

    return (xp.reshape(bp, sp, d), xs.reshape(bs, ts, d),
            kp.reshape(1, bp, sp, nh, HEAD_DIM), vp.reshape(1, bp, sp, nh, HEAD_DIM), lfp.reshape(1, bp, sp, nh),
            lp_re[None], lp_im[None],
            ks.reshape(1, bs, ts, nh, HEAD_DIM), vs.reshape(1, bs, ts, nh, HEAD_DIM), lfs.reshape(1, bs, ts, nh),
            ls_re.transpose(1, 0, 2)[None], ls_im.transpose(1, 0, 2)[None])
```

```python
import functools
import math

import jax
import jax.numpy as jnp
import numpy as np
from jax import lax
from jax.experimental import pallas as pl
from jax.experimental.pallas import tpu as pltpu

F32 = jnp.float32
BF16 = jnp.bfloat16

EPS = 1e-6
NEG_INF = -1e30
LOG2E = math.log2(math.e)
HEAD_DIM = 128
SSM_GROUP = 16
SSM_STATE = 64
LANES = 128
MXU_COLS = 256
CHUNK_T = LANES // SSM_GROUP
GROUPS_PER_STEP = 8
GROUPS_PER_GRID_STEP = 2
VMEM_LIMIT = 56 * 1024 * 1024


def _cparams(sem):
    return pltpu.CompilerParams(dimension_semantics=sem, vmem_limit_bytes=VMEM_LIMIT)


def _rms(x, g):
    ms = jnp.mean(x * x, axis=-1, keepdims=True)
    return x * lax.rsqrt(ms + EPS) * g


def _dot(a, b):
    return jnp.dot(a, b, preferred_element_type=F32)


def _dot_nt(a, b):
    return lax.dot_general(a, b, (((1,), (1,)), ((), ())), preferred_element_type=F32)


def _ffn_up_kernel(x_ref, g_ref, w1_ref, w3_ref, *rest, n_cast):
    cast_in, act_ref, cast_out, h_ref = rest[:n_cast], rest[n_cast], rest[n_cast + 1:-1], rest[-1]

    @pl.when(pl.program_id(1) == 0)
    def _():
        h_ref[...] = _rms(x_ref[...], g_ref[...]).astype(BF16)

    h = h_ref[...]
    a = _dot(h, w1_ref[...])
    b = _dot(h, w3_ref[...])
    act_ref[...] = (0.5 * a * jax.nn.sigmoid(a) * b).astype(BF16)

    for src, dst in zip(cast_in, cast_out):
        dst[...] = src[...].astype(BF16)


def _ffn_down_kernel(act_ref, w2_ref, x_ref, o_ref):
    o_ref[...] = x_ref[...] + _dot(act_ref[...], w2_ref[...])


def _ffn_up(x, g, w1, w3, *, tm, tf, cast=(), cast_by="tiles"):
    t, d = x.shape
    f = w1.shape[1]
    tm, tf = min(tm, t), min(tf, f)
    ni, nj = t // tm, f // tf
    cast_specs = []
    for m in cast:
        r, c = m.shape
        if cast_by == "tiles":
            assert r % ni == 0 and c % nj == 0
            cast_specs.append(pl.BlockSpec((r // ni, c // nj), lambda i, j: (i, j)))
        else:
            assert r % (ni * nj) == 0
            cast_specs.append(pl.BlockSpec((r // (ni * nj), c), lambda i, j: (i * nj + j, 0)))
    outs = pl.pallas_call(
        functools.partial(_ffn_up_kernel, n_cast=len(cast)),
        grid=(ni, nj),
        in_specs=[
            pl.BlockSpec((tm, d), lambda i, j: (i, 0), pipeline_mode=pl.Buffered(1)),
            pl.BlockSpec((1, d), lambda i, j: (0, 0)),
            pl.BlockSpec((d, tf), lambda i, j: (0, j)),
            pl.BlockSpec((d, tf), lambda i, j: (0, j)),
        ] + cast_specs,
        out_specs=[pl.BlockSpec((tm, tf), lambda i, j: (i, j))] + cast_specs,
        out_shape=[jax.ShapeDtypeStruct((t, f), BF16)] + [jax.ShapeDtypeStruct(m.shape, BF16) for m in cast],
        scratch_shapes=[pltpu.VMEM((tm, d), BF16)],
        compiler_params=_cparams(("parallel", "arbitrary")),
        name="ffn_up",
    )(x, g.reshape(1, d), w1, w3, *cast)
    return outs[0], outs[1:]


def _ffn_down(act, w2, x, *, tm, tn):
    t, d = x.shape
    f = act.shape[1]
    tm, tn = min(tm, t), min(tn, d)
    return pl.pallas_call(
        _ffn_down_kernel,
        grid=(t // tm, d // tn),
        in_specs=[
            pl.BlockSpec((tm, f), lambda i, n: (i, 0)),
            pl.BlockSpec((f, tn), lambda i, n: (0, n)),
            pl.BlockSpec((tm, tn), lambda i, n: (i, n)),
        ],
        out_specs=pl.BlockSpec((tm, tn), lambda i, n: (i, n)),
        out_shape=jax.ShapeDtypeStruct((t, d), F32),
        compiler_params=_cparams(("parallel", "arbitrary")),
        name="ffn_down",
    )(act, w2, x)


def _inproj_kernel(x_ref, g_ref, w_ref, wf_ref, bf_ref, qn_ref, kn_ref,
                   q_ref, k_ref, kb_ref, v_ref, vb_ref, u_ref, lf_ref, h_ref, *, tiles):
    n = pl.program_id(1)

    @pl.when(n == 0)
    def _():
        h = _rms(x_ref[...], g_ref[...]).astype(BF16)
        h_ref[...] = h
        zf = _dot_nt(h, wf_ref[...]) + bf_ref[...]
        lf_ref[...] = jnp.minimum(zf, 0.0) - jnp.log1p(jnp.exp(-jnp.abs(zf)))

    sec = n // tiles
    tn = w_ref.shape[0]
    cw = min(MXU_COLS, tn)

    def chunks(weights):
        for c in range(tn // cw):
            cols = slice(c * cw, (c + 1) * cw)
            yield cols, _dot_nt(h_ref[...], weights[cols, :])

    def head_norm(z, gain):
        outs = [_rms(z[:, hh * HEAD_DIM:(hh + 1) * HEAD_DIM], gain) for hh in range(cw // HEAD_DIM)]
        return outs[0] if len(outs) == 1 else jnp.concatenate(outs, axis=-1)

    @pl.when(sec == 0)
    def _():
        for cols, z in chunks(w_ref):
            q_ref[:, cols] = (head_norm(z, qn_ref[...]) * (HEAD_DIM ** -0.5 * LOG2E)).astype(BF16)

    @pl.when(sec == 1)
    def _():
        for cols, z in chunks(w_ref):
            kk = head_norm(z, kn_ref[...])
            k_ref[:, cols] = kk
            kb_ref[:, cols] = kk.astype(BF16)

    @pl.when(sec == 2)
    def _():
        for cols, z in chunks(w_ref):
            v_ref[:, cols] = z
            vb_ref[:, cols] = z.astype(BF16)

    @pl.when(sec >= 3)
    def _():
        for cols, z in chunks(w_ref):
            u_ref[:, cols] = z


def _inproj(x, g, w_main, w_f, b_f, q_norm, k_norm, *, da, tm, tn):
    t, d = x.shape
    du = w_main.shape[0] - 3 * da
    nh = w_f.shape[0]
    tm, tn = min(tm, t), min(tn, da, du)
    tiles, tiles_u = da // tn, du // tn

    def sec_map(s, count):
        return lambda i, n: (i, jnp.clip(n - s * tiles, 0, count - 1))

    row = lambda i, n: (0, 0)
    attn_out = lambda dt: jax.ShapeDtypeStruct((t, da), dt)
    return pl.pallas_call(
        functools.partial(_inproj_kernel, tiles=tiles),
        grid=(t // tm, 3 * tiles + tiles_u),
        in_specs=[
            pl.BlockSpec((tm, d), lambda i, n: (i, 0), pipeline_mode=pl.Buffered(1)),
            pl.BlockSpec((1, d), row),
            pl.BlockSpec((tn, d), lambda i, n: (n, 0)),
            pl.BlockSpec((nh, d), row),
            pl.BlockSpec((1, nh), row),
            pl.BlockSpec((1, HEAD_DIM), row),
            pl.BlockSpec((1, HEAD_DIM), row),
        ],
        out_specs=[
            pl.BlockSpec((tm, tn), sec_map(0, tiles)),
            pl.BlockSpec((tm, tn), sec_map(1, tiles)),
            pl.BlockSpec((tm, tn), sec_map(1, tiles)),
            pl.BlockSpec((tm, tn), sec_map(2, tiles)),
            pl.BlockSpec((tm, tn), sec_map(2, tiles)),
            pl.BlockSpec((tm, tn), sec_map(3, tiles_u)),
            pl.BlockSpec((tm, nh), lambda i, n: (i, 0)),
        ],
        out_shape=[attn_out(BF16), attn_out(F32), attn_out(BF16), attn_out(F32), attn_out(BF16),
                   jax.ShapeDtypeStruct((t, du), F32), jax.ShapeDtypeStruct((t, nh), F32)],
        scratch_shapes=[pltpu.VMEM((tm, d), BF16)],
        compiler_params=_cparams(("parallel", "arbitrary")),
        name="inproj",
    )(x, g.reshape(1, d), w_main, w_f, b_f.reshape(1, nh),
      q_norm.reshape(1, HEAD_DIM), k_norm.reshape(1, HEAD_DIM))


def _split3(x):
    hi = x.astype(BF16)
    r1 = x - hi.astype(F32)
    mid = r1.astype(BF16)
    lo = (r1 - mid.astype(F32)).astype(BF16)
    return hi, mid, lo


def _cumsum_kernel(x_ref, o_ref, carry_ref):
    @pl.when(pl.program_id(0) == 0)
    def _():
        carry_ref[...] = jnp.zeros_like(carry_ref)

    x = x_ref[...]
    bs = x.shape[1]
    row = lax.broadcasted_iota(jnp.int32, (bs, bs), 0)
    col = lax.broadcasted_iota(jnp.int32, (bs, bs), 1)
    tri = jnp.where(row <= col, 1.0, 0.0).astype(BF16)
    hi, mid, lo = _split3(x)
    c = _dot(hi, tri) + _dot(mid, tri) + _dot(lo, tri) + carry_ref[...]
    o_ref[...] = c
    carry_ref[...] = c[:, bs - 1:bs]


def _cumsum_rows(x, *, bs):
    r, s = x.shape
    bs = min(bs, s)
    return pl.pallas_call(
        _cumsum_kernel,
        grid=(s // bs,),
        in_specs=[pl.BlockSpec((r, bs), lambda j: (0, j))],
        out_specs=pl.BlockSpec((r, bs), lambda j: (0, j)),
        out_shape=jax.ShapeDtypeStruct((r, s), F32),
        scratch_shapes=[pltpu.VMEM((r, 1), F32)],
        compiler_params=_cparams(("arbitrary",)),
        name="cumsum",
    )(x)


def _lane_tile(x, width):
    return x if width == LANES else jnp.tile(x, (1, width // LANES))


def _attn_prompt_kernel(qi_tab, ki_tab, q_ref, k_ref, v_ref, ccol_ref, crow_ref, o_ref,
                        m_ref, l_ref, acc_ref, cq_ref, *, heads):
    t = pl.program_id(2)
    qi, ki = qi_tab[t], ki_tab[t]
    tq, tk = q_ref.shape[1], k_ref.shape[1]

    @pl.when(ki == 0)
    def _():
        m_ref[...] = jnp.full_like(m_ref, NEG_INF)
        l_ref[...] = jnp.zeros_like(l_ref)
        acc_ref[...] = jnp.zeros_like(acc_ref)
        for hh in range(heads):
            cq_ref[hh] = jnp.broadcast_to(ccol_ref[0, hh] * LOG2E, (tq, LANES))

    def step(masked):
        for hh in range(heads):
            sl = slice(hh * HEAD_DIM, (hh + 1) * HEAD_DIM)
            s = _dot_nt(q_ref[0, :, sl], k_ref[0, :, sl])
            s = s + _lane_tile(cq_ref[hh], tk) - crow_ref[0, hh] * LOG2E
            if masked:
                row = lax.broadcasted_iota(jnp.int32, (tq, tk), 0)
                col = lax.broadcasted_iota(jnp.int32, (tq, tk), 1)
                s = jnp.where(col <= row, s, NEG_INF)
            m_prev = m_ref[hh]
            m_new = jnp.maximum(m_prev, jnp.max(s, axis=-1, keepdims=True))
            alpha = jnp.exp2(m_prev - m_new)
            p = jnp.exp2(s - _lane_tile(m_new, tk))
            l_ref[hh] = alpha * l_ref[hh] + jnp.sum(p, axis=-1, keepdims=True)
            acc_ref[hh] = alpha * acc_ref[hh] + _dot(p.astype(BF16), v_ref[0, :, sl])
            m_ref[hh] = m_new

    @pl.when(ki < qi)
    def _():
        step(False)

    @pl.when(ki == qi)
    def _():
        step(True)
        for hh in range(heads):
            o_ref[0, :, hh * HEAD_DIM:(hh + 1) * HEAD_DIM] = acc_ref[hh] / l_ref[hh]


def _attn_prompt(q, kb, vb, c_t, *, tq, heads):
    b, s, da = q.shape
    nh = da // HEAD_DIM
    tq = min(tq, s)
    heads = min(heads, nh)
    nq = s // tq
    pairs = [(i, j) for i in range(nq) for j in range(i + 1)]
    qi_tab = jnp.asarray([p[0] for p in pairs], jnp.int32)
    ki_tab = jnp.asarray([p[1] for p in pairs], jnp.int32)
    w = heads * HEAD_DIM
    grid_spec = pltpu.PrefetchScalarGridSpec(
        num_scalar_prefetch=2,
        grid=(b, nh // heads, len(pairs)),
        in_specs=[
            pl.BlockSpec((1, tq, w), lambda bi, hg, t, qt, kt: (bi, qt[t], hg)),
            pl.BlockSpec((1, tq, w), lambda bi, hg, t, qt, kt: (bi, kt[t], hg)),
            pl.BlockSpec((1, tq, w), lambda bi, hg, t, qt, kt: (bi, kt[t], hg)),
            pl.BlockSpec((1, heads, tq, 1), lambda bi, hg, t, qt, kt: (bi, hg, qt[t], 0)),
            pl.BlockSpec((1, heads, 1, tq), lambda bi, hg, t, qt, kt: (bi, hg, 0, kt[t])),
        ],
        out_specs=pl.BlockSpec((1, tq, w), lambda bi, hg, t, qt, kt: (bi, qt[t], hg)),
        scratch_shapes=[
            pltpu.VMEM((heads, tq, LANES), F32),
            pltpu.VMEM((heads, tq, LANES), F32),
            pltpu.VMEM((heads, tq, HEAD_DIM), F32),
            pltpu.VMEM((heads, tq, LANES), F32),
        ],
    )
    return pl.pallas_call(
        functools.partial(_attn_prompt_kernel, heads=heads),
        grid_spec=grid_spec,
        out_shape=jax.ShapeDtypeStruct((b, s, da), F32),
        compiler_params=_cparams(("parallel", "parallel", "arbitrary")),
        name="attn_prompt",
    )(qi_tab, ki_tab, q, kb, vb, c_t.reshape(b, nh, s, 1), c_t.reshape(b, nh, 1, s))


def _attn_sample_kernel(q_ref, ck_ref, cv_ref, kn_ref, vn_ref, crow_ref, ccol_ref, o_ref, kq_ref, vq_ref):
    t = q_ref.shape[1]
    nh = q_ref.shape[2] // HEAD_DIM
    past = ck_ref.shape[1] // nh
    row = lax.broadcasted_iota(jnp.int32, (t, t), 0)
    col = lax.broadcasted_iota(jnp.int32, (t, t), 1)

    s1 = kq_ref.shape[0]
    s2 = nh // s1
    for r in range(s1):
        kq_ref[r] = ck_ref[0, pl.ds(r, past * s2, stride=s1), :]
        vq_ref[r] = cv_ref[0, pl.ds(r, past * s2, stride=s1), :]

    for hh in range(nh):
        sl = slice(hh * HEAD_DIM, (hh + 1) * HEAD_DIM)
        r, a = hh % s1, hh // s1
        cached = pl.ds(a, past, stride=s2) if s2 > 1 else slice(None)
        qh = q_ref[0, :, sl]
        cq = ccol_ref[0, :, hh:hh + 1] * LOG2E
        ck = crow_ref[0, hh:hh + 1, :] * LOG2E
        sc = _dot_nt(qh, kq_ref[r, cached, :].astype(BF16)) + cq - ck[:, :past]
        sn = _dot_nt(qh, kn_ref[0, :, sl]) + cq - ck[:, past:]
        sn = jnp.where(col <= row, sn, NEG_INF)
        m = jnp.maximum(jnp.max(sc, axis=-1, keepdims=True), jnp.max(sn, axis=-1, keepdims=True))
        pc = jnp.exp2(sc - m)
        pn = jnp.exp2(sn - m)
        l = jnp.sum(pc, axis=-1, keepdims=True) + jnp.sum(pn, axis=-1, keepdims=True)
        o = _dot(pc.astype(BF16), vq_ref[r, cached, :].astype(BF16)) + _dot(pn.astype(BF16), vn_ref[0, :, sl])
        o_ref[0, :, sl] = o / l


def _attn_sample(q, cache_k, cache_v, kb, vb, c_t, c_col):
    b, t, da = q.shape
    nh = da // HEAD_DIM
    rows = cache_k.shape[1]
    first_stride = math.gcd(nh, 4)
    blk = lambda *shape: pl.BlockSpec((1,) + shape, lambda i: (i, 0, 0))
    return pl.pallas_call(
        _attn_sample_kernel,
        grid=(b,),
        in_specs=[blk(t, da), blk(rows, HEAD_DIM), blk(rows, HEAD_DIM), blk(t, da), blk(t, da),
                  blk(nh, rows // nh + t), blk(t, nh)],
        out_specs=blk(t, da),
        out_shape=jax.ShapeDtypeStruct((b, t, da), F32),
        scratch_shapes=[pltpu.VMEM((first_stride, rows // first_stride, HEAD_DIM), F32)] * 2,
        compiler_params=_cparams(("parallel",)),
        name="attn_sample",
    )(q, cache_k, cache_v, kb, vb, c_t, c_col)


def _cexp(mag_log, ang):
    mag = jnp.exp(mag_log)
    return mag * jnp.cos(ang), mag * jnp.sin(ang)


def _cmul(ar, ai, br, bi):
    return ar * br - ai * bi, ar * bi + ai * br


def _cpow_table(zr, zi, k):
    acc_r = jnp.broadcast_to(zr, k.shape)
    acc_i = jnp.broadcast_to(zi, k.shape)
    bit = 1
    while bit < CHUNK_T:
        take = (k & bit) != 0
        acc_r, acc_i = _cmul(acc_r, acc_i, jnp.where(take, zr, 1.0), jnp.where(take, zi, 0.0))
        zr, zi = _cmul(zr, zi, zr, zi)
        bit *= 2
    return acc_r, acc_i


def _gelu_tanh(x):
    return 0.5 * x * (1.0 + jnp.tanh(math.sqrt(2.0 / math.pi) * (x + 0.044715 * (x * x * x))))


def _split2(x):
    hi = x.astype(BF16)
    lo = (x - hi.astype(F32)).astype(BF16)
    return hi, lo


def _lane_block(rows):
    return lax.broadcasted_iota(jnp.int32, (rows, LANES), 1) // SSM_GROUP


def _block_transpose(xs):
    blk = _lane_block(xs[0].shape[0])
    s = CHUNK_T // 2
    while s:
        upper = (blk & s) != 0
        nxt = list(xs)
        for a in range(CHUNK_T):
            if a & s:
                continue
            lo, hi = xs[a], xs[a + s]
            nxt[a] = jnp.where(upper, pltpu.roll(hi, s * SSM_GROUP, axis=1), lo)
            nxt[a + s] = jnp.where(upper, hi, pltpu.roll(lo, (CHUNK_T - s) * SSM_GROUP, axis=1))
        xs = nxt
        s //= 2
    return xs


def _to_chunk_layout(src_ref, dst_ref):
    rows = dst_ref.shape[1]
    toks = [src_ref[pl.ds(t, rows, stride=CHUNK_T), :] for t in range(CHUNK_T)]
    for i, x in enumerate(_block_transpose(toks)):
        dst_ref[i] = x


def _from_chunk_layout(src_ref, dst_ref):
    rows = src_ref.shape[1]
    for t, x in enumerate(_block_transpose([src_ref[i] for i in range(CHUNK_T)])):
        dst_ref[pl.ds(t, rows, stride=CHUNK_T), :] = x


def _ssm_kernel(up_ref, us_ref, x0r_ref, x0i_ref, lrow_re_ref, lrow_im_ref, lcol_re_ref, lcol_im_ref,
                ldt_ref, l8_re_ref, l8_im_ref, ldt8_ref, bt_re_ref, bt_im_ref, ct_re_ref, ct_im_ref, d_ref,
                yp_ref, ys_ref, lastp_re_ref, lastp_im_ref, lasts_re_ref, lasts_im_ref,
                cp_ref, cs_ref, wr_ref, wi_ref, xr_ref, xi_ref, wsr_ref, wsi_ref, xsr_ref, xsi_ref,
                vre_ref, vim_ref, *, streams_p):
    j = pl.program_id(1)
    gb, gs = GROUPS_PER_STEP, GROUPS_PER_GRID_STEP
    rp = cp_ref.shape[1]
    p = SSM_STATE

    @pl.when(j == 0)
    def _():
        _to_chunk_layout(up_ref, cp_ref)
        _to_chunk_layout(us_ref, cs_ref)

    for jj in range(gs):
        gl = j * gs + jj
        dt = jnp.exp(ldt_ref[jj])

        a_row = lrow_re_ref[jj] * dt
        w_row = lrow_im_ref[jj] * dt
        s_idx = lax.broadcasted_iota(jnp.int32, (LANES, p), 0) // SSM_GROUP
        en_re, en_im = _cpow_table(*_cexp(-a_row, -w_row), s_idx)
        l8_re, l8_im = _cexp(CHUNK_T * a_row, CHUNK_T * w_row)
        lb_re, lb_im = _cexp(a_row, w_row)
        lr, li = lrow_re_ref[jj], lrow_im_ref[jj]
        inv = 1.0 / (lr * lr + li * li)
        cf_re, cf_im = _cmul(lb_re - 1.0, lb_im, lr * inv, -li * inv)
        bb_re, bb_im = _cmul(cf_re, cf_im, bt_re_ref[jj], bt_im_ref[jj])
        f_re, f_im = _cmul(en_re, en_im, bb_re, bb_im)
        e7_re, e7_im = _cmul(en_re, en_im, l8_re, l8_im)
        w_re, w_im = _cmul(e7_re, e7_im, bb_re, bb_im)
        w_re, w_im = w_re.astype(BF16), w_im.astype(BF16)

        a_col = lcol_re_ref[jj] * dt
        w_col = lcol_im_ref[jj] * dt
        t_idx = lax.broadcasted_iota(jnp.int32, (p, LANES), 1) // SSM_GROUP
        et_re, et_im = _cpow_table(*_cexp(a_col, w_col), t_idx)
        g_re, g_im = _cmul(et_re, et_im, ct_re_ref[jj], ct_im_ref[jj])
        v_re, v_im = g_re.astype(BF16), (-g_im).astype(BF16)
        vre_ref[gl] = v_re
        vim_ref[gl] = v_im

        fs = jnp.concatenate([f_re, -f_im], axis=1)
        hs = jnp.concatenate([g_re, g_im], axis=0)
        fh, fl = _split2(fs)
        hh, hl = _split2(hs)
        tmat = _dot(fh, hh) + _dot(fh, hl) + _dot(fl, hh)
        srow = lax.broadcasted_iota(jnp.int32, (LANES, LANES), 0) // SSM_GROUP
        tcol = lax.broadcasted_iota(jnp.int32, (LANES, LANES), 1) // SSM_GROUP
        tmat = jnp.where(srow <= tcol, tmat, 0.0).astype(BF16)

        d_row = d_ref[jj]

        u = cp_ref[gl]
        ub = u.astype(BF16)
        cp_ref[gl] = _dot(ub, tmat) + d_row * u
        base = pl.multiple_of(gl * rp, 8)
        wr_ref[pl.ds(base, rp), :] = _dot(ub, w_re)
        wi_ref[pl.ds(base, rp), :] = _dot(ub, w_im)

        us = cs_ref[gl]
        usb = us.astype(BF16)
        wsr_ref[...] = _dot(usb, w_re)
        wsi_ref[...] = _dot(usb, w_im)
        nb = x0r_ref.shape[1]
        ncs = us.shape[0] // nb
        sr, si = x0r_ref[jj], x0i_ref[jj]
        for c in range(ncs):
            rows = pl.ds(c, nb, stride=ncs)
            xsr_ref[rows, :] = sr
            xsi_ref[rows, :] = si
            nr, ni = _cmul(l8_re, l8_im, sr, si)
            sr = nr + wsr_ref[rows, :]
            si = ni + wsi_ref[rows, :]
        lasts_re_ref[jj] = sr
        lasts_im_ref[jj] = si
        xs_re = xsr_ref[...].astype(BF16)
        xs_im = xsi_ref[...].astype(BF16)
        cs_ref[gl] = _gelu_tanh(_dot(usb, tmat) + _dot(xs_re, v_re) + _dot(xs_im, v_im) + d_row * us)

    @pl.when(j == gb // gs - 1)
    def _():
        dt8 = jnp.exp(ldt8_ref[0])
        a8_re, a8_im = _cexp(CHUNK_T * l8_re_ref[0] * dt8, CHUNK_T * l8_im_ref[0] * dt8)
        nc = rp // streams_p

        def body(c, carry):
            new = []
            for b in range(streams_p):
                sr, si = carry[2 * b], carry[2 * b + 1]
                r = b * nc + c
                xr_ref[pl.ds(r, gb, stride=rp), :] = sr
                xi_ref[pl.ds(r, gb, stride=rp), :] = si
                nr, ni = _cmul(a8_re, a8_im, sr, si)
                new.append(nr + wr_ref[pl.ds(r, gb, stride=rp), :])
                new.append(ni + wi_ref[pl.ds(r, gb, stride=rp), :])
            return tuple(new)

        zero = jnp.zeros((gb, p), F32)
        fin = lax.fori_loop(0, nc, body, (zero,) * (2 * streams_p), unroll=4)
        for b in range(streams_p):
            lastp_re_ref[b] = fin[2 * b]
            lastp_im_ref[b] = fin[2 * b + 1]

        for g in range(gb):
            xr = xr_ref[g * rp:(g + 1) * rp, :].astype(BF16)
            xi = xi_ref[g * rp:(g + 1) * rp, :].astype(BF16)
            cp_ref[g] = _gelu_tanh(cp_ref[g] + _dot(xr, vre_ref[g]) + _dot(xi, vim_ref[g]))

        _from_chunk_layout(cp_ref, yp_ref)
        _from_chunk_layout(cs_ref, ys_ref)


def _ssm(up, us, x0_re, x0_im, lam_re, lam_im, log_dt, bt_re, bt_im, ct_re, ct_im, d_t, *, streams_p):
    tp, dssm = up.shape
    ts = us.shape[0]
    g = dssm // SSM_GROUP
    rp, rs = tp // CHUNK_T, ts // CHUNK_T
    nb = x0_re.shape[1]
    p = SSM_STATE
    gb, gs = GROUPS_PER_STEP, GROUPS_PER_GRID_STEP
    ngb = g // gb
    per_group = lambda *shape: pl.BlockSpec((gs,) + shape, lambda i, j: (i * (gb // gs) + j, 0, 0))
    per_block = lambda *shape: pl.BlockSpec((1,) + shape, lambda i, j: (i, 0, 0))
    tokens = lambda rows, **kw: pl.BlockSpec((rows, LANES), lambda i, j: (0, i), **kw)
    return pl.pallas_call(
        functools.partial(_ssm_kernel, streams_p=streams_p),
        grid=(ngb, gb // gs),
        in_specs=[
            tokens(tp, pipeline_mode=pl.Buffered(1)), tokens(ts, pipeline_mode=pl.Buffered(1)),
            per_group(nb, p), per_group(nb, p),
            per_group(1, p), per_group(1, p), per_group(p, 1), per_group(p, 1), per_group(1, 1),
            per_block(gb, p), per_block(gb, p), per_block(gb, 1),
            per_group(LANES, p), per_group(LANES, p), per_group(p, LANES), per_group(p, LANES),
            per_group(1, LANES),
        ],
        out_specs=[
            tokens(tp), tokens(ts),
            pl.BlockSpec((streams_p, gb, p), lambda i, j: (0, i, 0)),
            pl.BlockSpec((streams_p, gb, p), lambda i, j: (0, i, 0)),
            per_group(nb, p), per_group(nb, p),
        ],
        out_shape=[
            jax.ShapeDtypeStruct((tp, dssm), F32),
            jax.ShapeDtypeStruct((ts, dssm), F32),
            jax.ShapeDtypeStruct((streams_p, g, p), F32),
            jax.ShapeDtypeStruct((streams_p, g, p), F32),
            jax.ShapeDtypeStruct((g, nb, p), F32),
            jax.ShapeDtypeStruct((g, nb, p), F32),
        ],
        scratch_shapes=[
            pltpu.VMEM((gb, rp, LANES), F32), pltpu.VMEM((gb, rs, LANES), F32),
            pltpu.VMEM((gb * rp, p), F32), pltpu.VMEM((gb * rp, p), F32),
            pltpu.VMEM((gb * rp, p), F32), pltpu.VMEM((gb * rp, p), F32),
            pltpu.VMEM((rs, p), F32), pltpu.VMEM((rs, p), F32),
            pltpu.VMEM((rs, p), F32), pltpu.VMEM((rs, p), F32),
            pltpu.VMEM((gb, p, LANES), BF16), pltpu.VMEM((gb, p, LANES), BF16),
        ],
        compiler_params=_cparams(("parallel", "arbitrary")),
        name="ssm",
    )(up, us, x0_re, x0_im,
      lam_re.reshape(g, 1, p), lam_im.reshape(g, 1, p), lam_re.reshape(g, p, 1), lam_im.reshape(g, p, 1),
      log_dt.reshape(g, 1, 1),
      lam_re.reshape(ngb, gb, p), lam_im.reshape(ngb, gb, p), log_dt.reshape(ngb, gb, 1),
      bt_re, bt_im, ct_re, ct_im, d_t)


def _glu_kernel(y_ref, w_ref, o_ref):
    y = y_ref[...]
    o_ref[...] = y * jax.nn.sigmoid(_dot(y.astype(BF16), w_ref[...]))


def _glu(y, w, *, tm):
    t, d = y.shape
    tm = min(tm, t)
    return pl.pallas_call(
        _glu_kernel,
        grid=(t // tm,),
        in_specs=[pl.BlockSpec((tm, d), lambda i: (i, 0)), pl.BlockSpec((d, d), lambda i: (0, 0))],
        out_specs=pl.BlockSpec((tm, d), lambda i: (i, 0)),
        out_shape=jax.ShapeDtypeStruct((t, d), F32),
        compiler_params=_cparams(("parallel",)),
        name="glu",
    )(y, w)


def _merge_kernel(o_ref, y_ref, x_ref, ga_ref, gs_ref, w_ref, out_ref, a_ref):
    da = o_ref.shape[1]

    @pl.when(pl.program_id(1) == 0)
    def _():
        a_ref[:, :da] = _rms(o_ref[...], ga_ref[...]).astype(BF16)
        a_ref[:, da:] = _rms(y_ref[...], gs_ref[...]).astype(BF16)

    out_ref[...] = x_ref[...] + _dot(a_ref[...], w_ref[...])


def _merge(o, y, x, g_attn, g_ssm, w_out, *, tm, tn):
    t, da = o.shape
    ds = y.shape[1]
    d = x.shape[1]
    tm, tn = min(tm, t), min(tn, d)
    return pl.pallas_call(
        _merge_kernel,
        grid=(t // tm, d // tn),
        in_specs=[
            pl.BlockSpec((tm, da), lambda i, n: (i, 0), pipeline_mode=pl.Buffered(1)),
            pl.BlockSpec((tm, ds), lambda i, n: (i, 0), pipeline_mode=pl.Buffered(1)),
            pl.BlockSpec((tm, tn), lambda i, n: (i, n)),
            pl.BlockSpec((1, da), lambda i, n: (0, 0)),
            pl.BlockSpec((1, ds), lambda i, n: (0, 0)),
            pl.BlockSpec((da + ds, tn), lambda i, n: (0, n)),
        ],
        out_specs=pl.BlockSpec((tm, tn), lambda i, n: (i, n)),
        out_shape=jax.ShapeDtypeStruct((t, d), F32),
        scratch_shapes=[pltpu.VMEM((tm, da + ds), BF16)],
        compiler_params=_cparams(("parallel", "arbitrary")),
        name="merge",
    )(o, y, x, g_attn.reshape(1, da), g_ssm.reshape(1, ds), w_out)


TILES = dict(
    ffn_up=dict(tm=1024, tf=256),
    ffn_down=dict(tm=512, tn=512),
    inproj=dict(tm=1024, tn=512),
    attn_prompt=dict(tq=512, heads=8),
    cumsum=dict(bs=512),
    glu=dict(tm=512),
    merge=dict(tm=1024, tn=512),
)


def kernel(x_prompt, x_sample, cache_k, cache_v, cache_logf, state_ssm_re, state_ssm_im, norm_ffn1, w1_a, w3_a, w2_a, norm_mix, w_in, b_f, q_norm, k_norm, lam_re, lam_im, log_dt, b_re, b_im, c_re, c_im, d_skip, w_glu, out_norm_attn, out_norm_ssm, w_out, norm_ffn2, w1_b, w3_b, w2_b):
    depth = norm_ffn1.shape[0]
    assert depth == 1
    l = 0
    bp, sp, d = x_prompt.shape
    bs, ts, _ = x_sample.shape
    past = cache_k.shape[2]
    nh = b_f.shape[1]
    da = nh * HEAD_DIM
    groups = lam_re.shape[1]
    dssm = groups * SSM_GROUP
    p = SSM_STATE

    bf = lambda w: w.astype(BF16)
    w_in_t = jnp.swapaxes(w_in[l], 0, 1)
    w_main = bf(jnp.concatenate([w_in_t[:3 * da], w_in_t[3 * da + nh:]], axis=0))
    w_f = bf(w_in_t[3 * da:3 * da + nh])

    xp = x_prompt.reshape(bp * sp, d)
    xs = x_sample.reshape(bs * ts, d)

    def ffn(xs, xp, g, w1, w3, w2_f32, later=()):
        act_s, (w2,) = _ffn_up(xs, g, w1, w3, cast=(w2_f32,), cast_by="slabs", **TILES["ffn_up"])
        xs = _ffn_down(act_s, w2, xs, **TILES["ffn_down"])
        act_p, later = _ffn_up(xp, g, w1, w3, cast=later, cast_by="tiles", **TILES["ffn_up"])
        xp = _ffn_down(act_p, w2, xp, **TILES["ffn_down"])
        return xs, xp, later

    xs, xp, (w1_b16, w3_b16) = ffn(xs, xp, norm_ffn1[l], bf(w1_a[l]), bf(w3_a[l]), w2_a[l],
                                   later=(w1_b[l], w3_b[l]))

    proj = functools.partial(_inproj, g=norm_mix[l], w_main=w_main, w_f=w_f, b_f=b_f[l], da=da,
                             q_norm=q_norm[l], k_norm=k_norm[l], **TILES["inproj"])
    qp, kp, kbp, vp, vbp, up, lfp = proj(xp)
    qs, ks, kbs, vs, vbs, us, lfs = proj(xs)

    lfp_t = lfp.reshape(bp, sp, nh).transpose(0, 2, 1).reshape(bp * nh, sp)
    cp_t = _cumsum_rows(lfp_t, **TILES["cumsum"]).reshape(bp, nh, sp)
    o_p = _attn_prompt(qp.reshape(bp, sp, da), kbp.reshape(bp, sp, da), vbp.reshape(bp, sp, da), cp_t,
                       **TILES["attn_prompt"])

    lfs_all = jnp.concatenate([cache_logf[l].astype(F32), lfs.reshape(bs, ts, nh)], axis=1)
    lfs_t = lfs_all.transpose(0, 2, 1).reshape(bs * nh, past + ts)
    cs_t = _cumsum_rows(lfs_t, bs=past + ts).reshape(bs, nh, past + ts)
    cs_col = cs_t[:, :, past:].transpose(0, 2, 1)
    o_s = _attn_sample(qs.reshape(bs, ts, da),
                       cache_k[l].reshape(bs, past * nh, HEAD_DIM), cache_v[l].reshape(bs, past * nh, HEAD_DIM),
                       kbs.reshape(bs, ts, da), vbs.reshape(bs, ts, da), cs_t, cs_col)

    bt = lambda b: jnp.tile(b.transpose(0, 2, 1), (1, CHUNK_T, 1))
    ct = lambda c: jnp.tile(c.transpose(0, 2, 1), (1, 1, CHUNK_T))
    d_t = jnp.tile(d_skip[l].reshape(groups, 1, SSM_GROUP), (1, 1, CHUNK_T))
    yp, ys, lp_re, lp_im, ls_re, ls_im = _ssm(
        up, us,
        state_ssm_re[l].astype(F32).transpose(1, 0, 2), state_ssm_im[l].astype(F32).transpose(1, 0, 2),
        lam_re[l], lam_im[l], log_dt[l], bt(b_re[l]), bt(b_im[l]), ct(c_re[l]), ct(c_im[l]), d_t,
        streams_p=bp)

    glu = functools.partial(_glu, w=bf(w_glu[l]), **TILES["glu"])
    merge = functools.partial(_merge, g_attn=out_norm_attn[l], g_ssm=out_norm_ssm[l], w_out=bf(w_out[l]),
                              **TILES["merge"])
    xp = merge(o_p.reshape(bp * sp, da), glu(yp), xp)
    xs = merge(o_s.reshape(bs * ts, da), glu(ys), xs)

    xs, xp, _ = ffn(xs, xp, norm_ffn2[l], w1_b16, w3_b16, w2_b[l])

    return (xp.reshape(bp, sp, d), xs.reshape(bs, ts, d),
            kp.reshape(1, bp, sp, nh, HEAD_DIM), vp.reshape(1, bp, sp, nh, HEAD_DIM), lfp.reshape(1, bp, sp, nh),
            lp_re[None], lp_im[None],
            ks.reshape(1, bs, ts, nh, HEAD_DIM), vs.reshape(1, bs, ts, nh, HEAD_DIM), lfs.reshape(1, bs, ts, nh),
            ls_re.transpose(1, 0, 2)[None], ls_im.transpose(1, 0, 2)[None])
```
